```python
import math
import jax
import jax.numpy as jnp
from jax import lax
import numpy as np

D_MODEL = 1024
BATCH = 8
SEQ = 4096
DEPTH = 2

A_HEADS = 4
A_DK = 64
A_DV = 64
B_HEADS = 8
B_GROUPS = 2
B_HPG = B_HEADS // B_GROUPS
B_DH = 64
C_HEADS = 4
C_DH = 64
MIX_WIDTH = A_HEADS * A_DV + B_HEADS * B_DH + C_HEADS * C_DH
D_FF = 4 * D_MODEL
ROPE_THETA = 500000.0
ROT_DIM = B_DH // 4
EPS = 1e-6
NEG_BIG = -1e30
POS_BIG = 1e30
TINY = 1e-30
HGRN_CHUNK = 64
CMP_BLOCK = 32
CMP_STRIDE = 16
CMP_HIDDEN = 128
SEL_BLOCK = 64
SEL_TOPK = 16
WINDOW = 512
NSA_QBLOCK = 64
FOX_QBLOCK = 128

IN_SPLIT = (
    A_HEADS * A_DK, A_HEADS * A_DK, A_HEADS * A_DV, A_HEADS * A_DV,
    B_HEADS * B_DH,
    B_GROUPS * B_DH, B_GROUPS * B_DH,
    B_GROUPS * B_DH, B_GROUPS * B_DH,
    B_GROUPS * B_DH, B_GROUPS * B_DH,
    3 * B_HEADS,
    C_HEADS * C_DH, C_HEADS * C_DH, C_HEADS * C_DH, C_HEADS,
)
IN_WIDTH = sum(IN_SPLIT)

kernel_name = "hybrid_hgrn2_nsa_fox_block"


def _rmsnorm(x, g):
    xf = x.astype(jnp.float32)
    var = jnp.mean(xf * xf, axis=-1, keepdims=True)
    return xf * lax.rsqrt(var + EPS) * g.astype(jnp.float32)


def _rope_partial(x, pos):
    half = ROT_DIM // 2
    inv_freq = jnp.power(jnp.float32(ROPE_THETA), -jnp.arange(0, ROT_DIM, 2, dtype=jnp.float32) / ROT_DIM)
    ang = pos.astype(jnp.float32)[:, None] * inv_freq[None, :]
    ang = ang.reshape((1, pos.shape[0]) + (1,) * (x.ndim - 3) + (half,))
    cos, sin = jnp.cos(ang), jnp.sin(ang)
    x1 = x[..., :half]
    x2 = x[..., half:ROT_DIM]
    return jnp.concatenate([x1 * cos - x2 * sin, x2 * cos + x1 * sin, x[..., ROT_DIM:]], axis=-1)


def _masked_softmax(logits, mask):
    logits = jnp.where(mask, logits.astype(jnp.float32), NEG_BIG)
    m = jnp.max(logits, axis=-1, keepdims=True)
    p = jnp.where(mask, jnp.exp(logits - m), 0.0)
    return p / jnp.maximum(jnp.sum(p, axis=-1, keepdims=True), TINY)


def _hgrn2(q, f_logit, i_val, g_out, lb, onorm_g):
    bsz, seq = q.shape[0], q.shape[1]
    lb = lb.reshape(A_HEADS, A_DK)
    z = f_logit.astype(jnp.float32).reshape(bsz, seq, A_HEADS, A_DK)
    f = lb + (1.0 - lb) * jax.nn.sigmoid(z)
    log_f = jnp.log(jnp.maximum(f, TINY))
    k = (1.0 - lb) * jax.nn.sigmoid(-z)
    qf = q.astype(jnp.float32).reshape(bsz, seq, A_HEADS, A_DK) * (A_DK ** -0.5)
    v = i_val.astype(jnp.float32).reshape(bsz, seq, A_HEADS, A_DV)
    n_chunks = seq // HGRN_CHUNK

    def to_chunks(t):
        return t.reshape(bsz, n_chunks, HGRN_CHUNK, A_HEADS, -1).transpose(1, 0, 3, 2, 4)

    causal = jnp.tril(jnp.ones((HGRN_CHUNK, HGRN_CHUNK), dtype=bool))

    def step(state, inp):
        qc, kc, vc, gc = inp
        G = jnp.cumsum(gc, axis=2)
        o_inter = jnp.einsum('bhtk,bhkv->bhtv', qc * jnp.exp(G), state)
        diff = G[:, :, :, None, :] - G[:, :, None, :, :]
        decay = jnp.exp(jnp.where(causal[:, :, None], diff, NEG_BIG))
        scores = jnp.einsum('bhtk,bhsk,bhtsk->bhts', qc, kc, decay)
        o_intra = jnp.einsum('bhts,bhsv->bhtv', scores, vc)
        g_last = G[:, :, -1:, :]
        k_dec = kc * jnp.exp(g_last - G)
        state = state * jnp.exp(g_last[:, :, 0, :])[..., None] + jnp.einsum('bhsk,bhsv->bhkv', k_dec, vc)
        return state, o_inter + o_intra

    state0 = jnp.zeros((bsz, A_HEADS, A_DK, A_DV), jnp.float32)
    _, o = lax.scan(step, state0, (to_chunks(qf), to_chunks(k), to_chunks(v), to_chunks(log_f)))
    o = o.transpose(1, 0, 3, 2, 4).reshape(bsz, seq, A_HEADS, A_DV)
    gate = jax.nn.silu(g_out.astype(jnp.float32)).reshape(bsz, seq, A_HEADS, A_DV)
    o = _rmsnorm(o, onorm_g) * gate
    return o.reshape(bsz, seq, A_HEADS * A_DV)


def _nsa(q, k_cmp, v_cmp, k_slc, v_slc, k_win, v_win, gates, qn_g, kn_g, cmp_pos, cmp_w1, cmp_w2):
    bsz, seq = q.shape[0], q.shape[1]
    pos = jnp.arange(seq)
    qh = _rope_partial(_rmsnorm(q.reshape(bsz, seq, B_GROUPS, B_HPG, B_DH), qn_g), pos) * (B_DH ** -0.5)
    ks = _rope_partial(_rmsnorm(k_slc.reshape(bsz, seq, B_GROUPS, B_DH), kn_g), pos)
    kw = _rope_partial(_rmsnorm(k_win.reshape(bsz, seq, B_GROUPS, B_DH), kn_g), pos)
    vs = v_slc.astype(jnp.float32).reshape(bsz, seq, B_GROUPS, B_DH)
    vw = v_win.astype(jnp.float32).reshape(bsz, seq, B_GROUPS, B_DH)

    n_sub = CMP_BLOCK // CMP_STRIDE
    n_cmp = seq // CMP_STRIDE - n_sub + 1

    def compress(t, which):
        sub = t.astype(jnp.float32).reshape(bsz, seq // CMP_STRIDE, CMP_STRIDE, B_GROUPS, B_DH)
        blocks = jnp.concatenate([sub[:, m:m + n_cmp] for m in range(n_sub)], axis=2)
        blocks = blocks + cmp_pos[which][None, None, :, None, :]
        flat = blocks.transpose(0, 1, 3, 2, 4).reshape(bsz, n_cmp, B_GROUPS, CMP_BLOCK * B_DH)
        hidden = jax.nn.gelu(flat @ cmp_w1[which])
        return hidden @ cmp_w2[which]

    cmp_end = jnp.arange(n_cmp) * CMP_STRIDE + CMP_BLOCK - 1
    kc = _rope_partial(_rmsnorm(compress(k_cmp.reshape(bsz, seq, B_GROUPS, B_DH), 0), kn_g), cmp_end)
    vc = compress(v_cmp.reshape(bsz, seq, B_GROUPS, B_DH), 1)

    n_sel = seq // SEL_BLOCK
    ci = np.arange(n_cmp)[:, None]
    sj = np.arange(n_sel)[None, :]
    c_start = ci * CMP_STRIDE
    overlap = jnp.asarray(((c_start <= sj * SEL_BLOCK + SEL_BLOCK - 1)
                           & (c_start + CMP_BLOCK - 1 >= sj * SEL_BLOCK)).astype(np.float32))
    topk = min(SEL_TOPK, n_sel)
    kb = ks.reshape(bsz, n_sel, SEL_BLOCK, B_GROUPS, B_DH).transpose(0, 3, 1, 2, 4)
    vb = vs.reshape(bsz, n_sel, SEL_BLOCK, B_GROUPS, B_DH).transpose(0, 3, 1, 2, 4)
    kw_pad = jnp.pad(kw, ((0, 0), (WINDOW, 0), (0, 0), (0, 0)))
    vw_pad = jnp.pad(vw, ((0, 0), (WINDOW, 0), (0, 0), (0, 0)))
    gate = jax.nn.sigmoid(gates.astype(jnp.float32)).reshape(bsz, seq, 3, B_GROUPS, B_HPG)
    b_idx = jnp.arange(bsz)[:, None, None, None]
    g_idx = jnp.arange(B_GROUPS)[None, :, None, None]
    sel_off = jnp.arange(SEL_BLOCK)
    win_off = jnp.arange(WINDOW + NSA_QBLOCK)
    blk_ids = jnp.arange(n_sel)

    def block_fn(blk):
        q0 = blk * NSA_QBLOCK
        t = q0 + jnp.arange(NSA_QBLOCK)
        qb = lax.dynamic_slice_in_dim(qh, q0, NSA_QBLOCK, axis=1)
        s_c = jnp.einsum('bqghd,bngd->bghqn', qb, kc)
        p_c = _masked_softmax(s_c, cmp_end[None, :] <= t[:, None])
        o_c = jnp.einsum('bghqn,bngd->bqghd', p_c, vc)
        imp = jnp.einsum('bghqn,nj->bgqj', p_c, overlap)
        cur = (t // SEL_BLOCK)[:, None]
        forced = (blk_ids[None, :] == 0) | (blk_ids[None, :] == cur) | (blk_ids[None, :] == cur - 1)
        imp = jnp.where(forced, POS_BIG, imp)
        imp = jnp.where(blk_ids[None, :] * SEL_BLOCK > t[:, None], NEG_BIG, imp)
        vals, idx = lax.top_k(imp, topk)
        k_sel = kb[b_idx, g_idx, idx]
        v_sel = vb[b_idx, g_idx, idx]
        s_s = jnp.einsum('bqghd,bgqkld->bghqkl', qb, k_sel).reshape(bsz, B_GROUPS, B_HPG, NSA_QBLOCK, topk * SEL_BLOCK)
        key_pos = idx[..., None] * SEL_BLOCK + sel_off
        m_s = (key_pos <= t[None, None, :, None, None]) & (vals > NEG_BIG * 0.5)[..., None]
        m_s = m_s.reshape(bsz, B_GROUPS, 1, NSA_QBLOCK, topk * SEL_BLOCK)
        p_s = _masked_softmax(s_s, m_s).reshape(bsz, B_GROUPS, B_HPG, NSA_QBLOCK, topk, SEL_BLOCK)
        o_s = jnp.einsum('bghqkl,bgqkld->bqghd', p_s, v_sel)
        kwb = lax.dynamic_slice_in_dim(kw_pad, q0, WINDOW + NSA_QBLOCK, axis=1)
        vwb = lax.dynamic_slice_in_dim(vw_pad, q0, WINDOW + NSA_QBLOCK, axis=1)
        kpos = q0 - WINDOW + win_off
        m_w = (kpos[None, :] >= 0) & (kpos[None, :] <= t[:, None]) & (kpos[None, :] > t[:, None] - WINDOW)
        s_w = jnp.einsum('bqghd,bkgd->bghqk', qb, kwb)
        p_w = _masked_softmax(s_w, m_w)
        o_w = jnp.einsum('bghqk,bkgd->bqghd', p_w, vwb)
        gb = lax.dynamic_slice_in_dim(gate, q0, NSA_QBLOCK, axis=1)
        return gb[:, :, 0, :, :, None] * o_c + gb[:, :, 1, :, :, None] * o_s + gb[:, :, 2, :, :, None] * o_w

    out = lax.map(block_fn, jnp.arange(seq // NSA_QBLOCK))
    return out.transpose(1, 0, 2, 3, 4, 5).reshape(bsz, seq, B_HEADS * B_DH)


def _fox(q, k, v, f_logit, qn_g, kn_g, fb):
    bsz, seq = q.shape[0], q.shape[1]
    qh = _rmsnorm(q.reshape(bsz, seq, C_HEADS, C_DH), qn_g) * (C_DH ** -0.5)
    kh = _rmsnorm(k.reshape(bsz, seq, C_HEADS, C_DH), kn_g)
    vh = v.astype(jnp.float32).reshape(bsz, seq, C_HEADS, C_DH)
    log_f = jax.nn.log_sigmoid(f_logit.astype(jnp.float32) + fb.astype(jnp.float32))
    c = jnp.cumsum(log_f, axis=1).transpose(0, 2, 1)
    kpos = jnp.arange(seq)

    def block_fn(blk):
        q0 = blk * FOX_QBLOCK
        t = q0 + jnp.arange(FOX_QBLOCK)
        qb = lax.dynamic_slice_in_dim(qh, q0, FOX_QBLOCK, axis=1)
        cq = lax.dynamic_slice_in_dim(c, q0, FOX_QBLOCK, axis=2)
        s = jnp.einsum('bqhd,bkhd->bhqk', qb, kh) + cq[..., None] - c[:, :, None, :]
        p = _masked_softmax(s, kpos[None, :] <= t[:, None])
        return jnp.einsum('bhqk,bkhd->bqhd', p, vh)

    out = lax.map(block_fn, jnp.arange(seq // FOX_QBLOCK))
    return out.transpose(1, 0, 2, 3, 4).reshape(bsz, seq, C_HEADS * C_DH)


def setup_inputs(seed: int = 0) -> dict:
    key = jax.random.key(seed)
    ks = jax.random.split(key, 17)

    def nrm(k, shape, scale):
        return jax.random.normal(k, shape, jnp.float32) * scale

    return {
        'x': nrm(ks[0], (BATCH, SEQ, D_MODEL), 1.0),
        'norm1_g': 1.0 + nrm(ks[1], (DEPTH, D_MODEL), 0.02),
        'w_in': nrm(ks[2], (DEPTH, D_MODEL, IN_WIDTH), D_MODEL ** -0.5),
        'hgrn_lb_logits': nrm(ks[3], (DEPTH, A_HEADS * A_DK), 0.1),
        'hgrn_onorm_g': 1.0 + nrm(ks[4], (DEPTH, A_DV), 0.02),
        'nsa_qn_g': 1.0 + nrm(ks[5], (DEPTH, B_DH), 0.02),
        'nsa_kn_g': 1.0 + nrm(ks[6], (DEPTH, B_DH), 0.02),
        'nsa_cmp_pos': nrm(ks[7], (DEPTH, 2, CMP_BLOCK, B_DH), 0.02),
        'nsa_cmp_w1': nrm(ks[8], (DEPTH, 2, CMP_BLOCK * B_DH, CMP_HIDDEN), (CMP_BLOCK * B_DH) ** -0.5),
        'nsa_cmp_w2': nrm(ks[9], (DEPTH, 2, CMP_HIDDEN, B_DH), CMP_HIDDEN ** -0.5),
        'fox_qn_g': 1.0 + nrm(ks[10], (DEPTH, C_DH), 0.02),
        'fox_kn_g': 1.0 + nrm(ks[11], (DEPTH, C_DH), 0.02),
        'fox_fb': 2.0 + nrm(ks[12], (DEPTH, C_HEADS), 0.1),
        'w_o': nrm(ks[13], (DEPTH, MIX_WIDTH, D_MODEL), MIX_WIDTH ** -0.5),
        'norm2_g': 1.0 + nrm(ks[14], (DEPTH, D_MODEL), 0.02),
        'w_up': nrm(ks[15], (DEPTH, D_MODEL, D_FF), D_MODEL ** -0.5),
        'w_down': nrm(ks[16], (DEPTH, D_FF, D_MODEL), D_FF ** -0.5),
    }


def reference(x, norm1_g, w_in, hgrn_lb_logits, hgrn_onorm_g, nsa_qn_g, nsa_kn_g, nsa_cmp_pos,
              nsa_cmp_w1, nsa_cmp_w2, fox_qn_g, fox_kn_g, fox_fb, w_o, norm2_g, w_up, w_down):
    lb_p = jax.nn.softmax(hgrn_lb_logits.astype(jnp.float32), axis=0)
    lb_all = jnp.cumsum(lb_p, axis=0) - lb_p[0:1]
    split_at = [int(v) for v in np.cumsum(IN_SPLIT)[:-1]]
    for layer in range(DEPTH):
        h = _rmsnorm(x, norm1_g[layer]).astype(x.dtype)
        proj = h @ w_in[layer]
        (aq, af, ai, ag, bq, bkc, bvc, bks, bvs, bkw, bvw, bg,
         cq, ck, cv, cf) = jnp.split(proj, split_at, axis=-1)
        o_a = _hgrn2(aq, af, ai, ag, lb_all[layer], hgrn_onorm_g[layer])
        o_b = _nsa(bq, bkc, bvc, bks, bvs, bkw, bvw, bg, nsa_qn_g[layer], nsa_kn_g[layer],
                   nsa_cmp_pos[layer], nsa_cmp_w1[layer], nsa_cmp_w2[layer])
        o_c = _fox(cq, ck, cv, cf, fox_qn_g[layer], fox_kn_g[layer], fox_fb[layer])
        mix = jnp.concatenate([o_a, o_b, o_c], axis=-1).astype(x.dtype)
        x = x + (mix @ w_o[layer]).astype(x.dtype)
        h2 = _rmsnorm(x, norm2_g[layer]).astype(x.dtype)
        x = x + (jnp.square(jax.nn.relu(h2 @ w_up[layer])) @ w_down[layer]).astype(x.dtype)
    return x
```

```python
import functools
import math

import numpy as np
import jax
import jax.numpy as jnp
from jax import lax
from jax.experimental import pallas as pl
from jax.experimental.pallas import tpu as pltpu

F32 = jnp.float32
BF16 = jnp.bfloat16

D_MODEL = 1024
DEPTH = 2
A_HEADS, A_DK, A_DV = 4, 64, 64
B_HEADS, B_GROUPS, B_DH = 8, 2, 64
B_HPG = B_HEADS // B_GROUPS
C_HEADS, C_DH = 4, 64
HEAD_DIM = 64
D_FF = 4 * D_MODEL
ROPE_THETA = 500000.0
ROT_DIM = B_DH // 4
ROT_HALF = ROT_DIM // 2
EPS = 1e-6
NEG_BIG = -1e30
POS_BIG = 1e30
TINY = 1e-30
CMP_BLOCK, CMP_STRIDE, CMP_HIDDEN = 32, 16, 128
SEL_BLOCK, SEL_TOPK, WINDOW = 64, 16, 512

A_WIDTH = A_HEADS * A_DK
BQ_WIDTH = B_HEADS * B_DH
BKV_WIDTH = B_GROUPS * B_DH
C_WIDTH = C_HEADS * C_DH
N_GATES = 3 * B_HEADS
GATE_LANES = 128
FOX_GATE_LANE0 = N_GATES

OFF_A = 0
OFF_BQ = 4 * A_WIDTH
OFF_KC = OFF_BQ + BQ_WIDTH
OFF_VC = OFF_KC + BKV_WIDTH
OFF_KS = OFF_VC + BKV_WIDTH
OFF_VS = OFF_KS + BKV_WIDTH
OFF_KW = OFF_VS + BKV_WIDTH
OFF_VW = OFF_KW + BKV_WIDTH
OFF_GATE = OFF_VW + BKV_WIDTH
OFF_CQ = OFF_GATE + GATE_LANES
OFF_CK = OFF_CQ + C_WIDTH
OFF_CV = OFF_CK + C_WIDTH
PROJ_WIDTH = OFF_CV + C_WIDTH

VMEM_LIMIT = 56 * 1024 * 1024


def _params(sem):
    return pltpu.CompilerParams(dimension_semantics=sem, vmem_limit_bytes=VMEM_LIMIT)


def _dot(a, b):
    return jnp.dot(a, b, preferred_element_type=F32)


def _dot_nt(a, b):
    return lax.dot_general(a, b, (((1,), (1,)), ((), ())), preferred_element_type=F32)


def _dot_tn(a, b):
    return lax.dot_general(a, b, (((0,), (0,)), ((), ())), preferred_element_type=F32)


def _split3(x):
    a = x.astype(BF16)
    r = x - a.astype(F32)
    b = r.astype(BF16)
    c = (r - b.astype(F32)).astype(BF16)
    return a, b, c


def _dot_exact_rhs(m01, x):
    a, b, c = _split3(x)
    return _dot(m01, a) + _dot(m01, b) + _dot(m01, c)


def _iota(shape, dim):
    return lax.broadcasted_iota(jnp.int32, shape, dim)


def _head_ones(width):
    r = _iota((width, width), 0) // HEAD_DIM
    c = _iota((width, width), 1) // HEAD_DIM
    return jnp.where(r == c, 1.0, 0.0).astype(BF16)


def _head_rmsnorm(x, head_ones):
    x2 = x * x
    hi = x2.astype(BF16)
    lo = (x2 - hi.astype(F32)).astype(BF16)
    ms = (_dot(hi, head_ones) + _dot(lo, head_ones)) * (1.0 / HEAD_DIM)
    return x * lax.rsqrt(ms + EPS)


def _rope(x, cos, sin_lo, sin_hi):
    n = x.shape[-1]
    return (x * cos + pltpu.roll(x, ROT_HALF, axis=1) * sin_lo
            + pltpu.roll(x, n - ROT_HALF, axis=1) * sin_hi)


def _sigmoid(x):
    return 1.0 / (1.0 + jnp.exp(-x))


def _inproj_body(x_ref, g_ref, w_ref, cos_ref, sl_ref, sh_ref, qg_ref, kg_ref, fqg_ref, fkg_ref,
                 hg_ref, nq_ref, nkv_ref, cmp_ref, gt_ref, fx_ref):
    x = x_ref[...]
    var = jnp.mean(x * x, axis=-1, keepdims=True)
    h = (x * lax.rsqrt(var + EPS) * g_ref[...]).astype(BF16)

    def proj(lo, hi):
        return _dot(h, w_ref[:, lo:hi])

    hg_ref[...] = proj(OFF_A, OFF_BQ)

    cos, sl, sh = cos_ref[...], sl_ref[...], sh_ref[...]
    ones_q = _head_ones(BQ_WIDTH)
    ones_kv = _head_ones(BKV_WIDTH)
    ones_c = _head_ones(C_WIDTH)

    q = _head_rmsnorm(proj(OFF_BQ, OFF_KC), ones_q) * qg_ref[...]
    q = _rope(q, cos, sl, sh) * (B_DH ** -0.5)
    for hd in range(B_HEADS):
        nq_ref[0, hd] = q[:, hd * B_DH:(hd + 1) * B_DH].astype(BF16)

    cmp_ref[0] = proj(OFF_KC, OFF_VC)
    cmp_ref[1] = proj(OFF_VC, OFF_KS)

    ckv, slkv, shkv = cos[:, :BKV_WIDTH], sl[:, :BKV_WIDTH], sh[:, :BKV_WIDTH]
    for slot, off, is_key in ((0, OFF_KS, True), (1, OFF_VS, False), (2, OFF_KW, True), (3, OFF_VW, False)):
        t = proj(off, off + BKV_WIDTH)
        if is_key:
            t = _rope(_head_rmsnorm(t, ones_kv) * kg_ref[...], ckv, slkv, shkv)
        for g in range(B_GROUPS):
            nkv_ref[0, slot * B_GROUPS + g] = t[:, g * B_DH:(g + 1) * B_DH].astype(BF16)

    gt_ref[...] = proj(OFF_GATE, OFF_CQ)

    fq = _head_rmsnorm(proj(OFF_CQ, OFF_CK), ones_c) * fqg_ref[...] * (C_DH ** -0.5)
    fk = _head_rmsnorm(proj(OFF_CK, OFF_CV), ones_c) * fkg_ref[...]
    fv = proj(OFF_CV, PROJ_WIDTH)
    for slot, t in enumerate((fq, fk, fv)):
        for hd in range(C_HEADS):
            fx_ref[0, slot * C_HEADS + hd] = t[:, hd * C_DH:(hd + 1) * C_DH].astype(BF16)


def _inproj(x2d, g, w, cos, sl, sh, qg, kg, fqg, fkg, bsz, seq, tb=256):
    n = bsz * seq
    nsb = seq // tb
    row = lambda i: (i, 0)
    const = lambda i: (0, 0)
    pos = lambda i: (i % nsb, 0)
    heads = lambda i: (i // nsb, 0, i % nsb, 0)
    return pl.pallas_call(
        _inproj_body,
        grid=(n // tb,),
        in_specs=[
            pl.BlockSpec((tb, D_MODEL), row),
            pl.BlockSpec((1, D_MODEL), const),
            pl.BlockSpec((D_MODEL, PROJ_WIDTH), const),
            pl.BlockSpec((tb, BQ_WIDTH), pos),
            pl.BlockSpec((tb, BQ_WIDTH), pos),
            pl.BlockSpec((tb, BQ_WIDTH), pos),
            pl.BlockSpec((1, BQ_WIDTH), const),
            pl.BlockSpec((1, BKV_WIDTH), const),
            pl.BlockSpec((1, C_WIDTH), const),
            pl.BlockSpec((1, C_WIDTH), const),
        ],
        out_specs=[
            pl.BlockSpec((tb, 4 * A_WIDTH), row),
            pl.BlockSpec((1, B_HEADS, tb, B_DH), heads),
            pl.BlockSpec((1, 4 * B_GROUPS, tb, B_DH), heads),
            pl.BlockSpec((2, tb, BKV_WIDTH), lambda i: (0, i, 0)),
            pl.BlockSpec((tb, GATE_LANES), row),
            pl.BlockSpec((1, 3 * C_HEADS, tb, C_DH), heads),
        ],
        out_shape=[
            jax.ShapeDtypeStruct((n, 4 * A_WIDTH), F32),
            jax.ShapeDtypeStruct((bsz, B_HEADS, seq, B_DH), BF16),
            jax.ShapeDtypeStruct((bsz, 4 * B_GROUPS, seq, B_DH), BF16),
            jax.ShapeDtypeStruct((2, n, BKV_WIDTH), F32),
            jax.ShapeDtypeStruct((n, GATE_LANES), F32),
            jax.ShapeDtypeStruct((bsz, 3 * C_HEADS, seq, C_DH), BF16),
        ],
        compiler_params=_params(("parallel",)),
        name="inproj",
    )(x2d, g, w, cos, sl, sh, qg, kg, fqg, fkg)


HGRN_SUB = 16
HGRN_ROWS = 128


def _hgrn_body(layer, hg_ref, lbl_ref, og_ref, o_ref, st_ref):
    @pl.when(pl.program_id(1) == 0)
    def _():
        st_ref[...] = jnp.zeros_like(st_ref)

    rows, width = HGRN_ROWS, A_WIDTH
    lbl = lbl_ref[...]
    e = jnp.exp(lbl - jnp.max(lbl, axis=0, keepdims=True))
    p = e / jnp.sum(e, axis=0, keepdims=True)
    lb = p[0:1] * 0.0
    for l in range(1, layer + 1):
        lb = lb + p[l:l + 1]

    q = hg_ref[0, :, 0:width] * (A_DK ** -0.5)
    z = hg_ref[0, :, width:2 * width]
    v = hg_ref[0, :, 2 * width:3 * width]
    go = hg_ref[0, :, 3 * width:4 * width]

    f = lb + (1.0 - lb) * _sigmoid(z)
    logf = jnp.log(jnp.maximum(f, TINY))
    k = (1.0 - lb) * _sigmoid(-z)

    r = _iota((rows, rows), 0)
    c = _iota((rows, rows), 1)
    tri = jnp.where((r // HGRN_SUB == c // HGRN_SUB) & (c <= r), 1.0, 0.0).astype(BF16)
    gl = _dot_exact_rhs(tri, logf)

    ones_h = _head_ones(width)
    tin = _iota((rows, width), 0) % HGRN_SUB

    o_band = jnp.zeros((rows, width), F32)
    for d in range(HGRN_SUB):
        if d == 0:
            ks, gs, vs = k, gl, v
        else:
            ks, gs, vs = (pltpu.roll(a, d, axis=0) for a in (k, gl, v))
        valid = tin >= d
        pr = jnp.where(valid, q * ks * jnp.exp(jnp.minimum(gl - gs, 0.0)), 0.0)
        o_band = o_band + _dot(pr.astype(BF16), ones_h) * vs

    hm = _iota((width, width), 0) // HEAD_DIM == _iota((width, width), 1) // HEAD_DIM
    st = st_ref[...]
    outs = []
    for i in range(rows // HGRN_SUB):
        sl_ = slice(i * HGRN_SUB, (i + 1) * HGRN_SUB)
        gi, qi, ki, vi = gl[sl_], q[sl_], k[sl_], v[sl_]
        outs.append(_dot_nt((qi * jnp.exp(gi)).astype(BF16), st.astype(BF16)))
        glast = gi[HGRN_SUB - 1:HGRN_SUB]
        kdec = ki * jnp.exp(glast - gi)
        upd = _dot_tn(vi.astype(BF16), kdec.astype(BF16))
        st = st * jnp.exp(glast) + jnp.where(hm, upd, 0.0)
    st_ref[...] = st
    o = jnp.concatenate(outs, axis=0) + o_band

    o = _head_rmsnorm(o, ones_h) * og_ref[...]
    gate = go * _sigmoid(go)
    o_ref[0] = (o * gate).astype(o_ref.dtype)


def _hgrn(hg3, lb_logits, og, layer):
    bsz, seq, _ = hg3.shape
    return pl.pallas_call(
        functools.partial(_hgrn_body, layer),
        grid=(bsz, seq // HGRN_ROWS),
        in_specs=[
            pl.BlockSpec((1, HGRN_ROWS, 4 * A_WIDTH), lambda b, i: (b, i, 0)),
            pl.BlockSpec((DEPTH, A_WIDTH), lambda b, i: (0, 0)),
            pl.BlockSpec((1, A_WIDTH), lambda b, i: (0, 0)),
        ],
        out_specs=pl.BlockSpec((1, HGRN_ROWS, A_WIDTH), lambda b, i: (b, i, 0)),
        out_shape=jax.ShapeDtypeStruct((bsz, seq, A_WIDTH), BF16),
        scratch_shapes=[pltpu.VMEM((A_WIDTH, A_WIDTH), F32)],
        compiler_params=_params(("parallel", "arbitrary")),
        name="hgrn2",
    )(hg3, lb_logits, og)


FOXC_ROWS = 512


def _foxc_body(gt_ref, fb_ref, ccol_ref, crow_ref, carry_ref):
    @pl.when(pl.program_id(1) == 0)
    def _():
        carry_ref[...] = jnp.zeros_like(carry_ref)

    rows = FOXC_ROWS
    y = gt_ref[0] + fb_ref[...]
    lf = jnp.minimum(y, 0.0) - jnp.log(1.0 + jnp.exp(-jnp.abs(y)))
    tri = jnp.where(_iota((rows, rows), 1) <= _iota((rows, rows), 0), 1.0, 0.0).astype(BF16)
    cc = _dot_exact_rhs(tri, lf) + carry_ref[...]
    carry_ref[...] = cc[rows - 1:rows]
    ccol_ref[0] = cc
    sel = jnp.where(_iota((8, GATE_LANES), 1) == _iota((8, GATE_LANES), 0) + FOX_GATE_LANE0, 1.0, 0.0).astype(BF16)
    a, b, c = _split3(cc)
    crow_ref[0] = _dot_nt(sel, a) + _dot_nt(sel, b) + _dot_nt(sel, c)


def _foxc(gates3, fb_row):
    bsz, seq, _ = gates3.shape
    return pl.pallas_call(
        _foxc_body,
        grid=(bsz, seq // FOXC_ROWS),
        in_specs=[
            pl.BlockSpec((1, FOXC_ROWS, GATE_LANES), lambda b, i: (b, i, 0)),
            pl.BlockSpec((1, GATE_LANES), lambda b, i: (0, 0)),
        ],
        out_specs=[
            pl.BlockSpec((1, FOXC_ROWS, GATE_LANES), lambda b, i: (b, i, 0)),
            pl.BlockSpec((1, 8, FOXC_ROWS), lambda b, i: (b, 0, i)),
        ],
        out_shape=[
            jax.ShapeDtypeStruct((bsz, seq, GATE_LANES), F32),
            jax.ShapeDtypeStruct((bsz, 8, seq), F32),
        ],
        scratch_shapes=[pltpu.VMEM((1, GATE_LANES), F32)],
        compiler_params=_params(("parallel", "arbitrary")),
        name="fox_cumgate",
    )(gates3, fb_row)


FOX_BLK = 512


def _fox_body(q_ref, k_ref, v_ref, ccol_ref, crow_ref, o_ref, m_ref, l_ref, acc_ref):
    qi, kj = pl.program_id(1), pl.program_id(2)
    blk = FOX_BLK

    @pl.when(kj == 0)
    def _():
        m_ref[...] = jnp.full_like(m_ref, NEG_BIG)
        l_ref[...] = jnp.zeros_like(l_ref)
        acc_ref[...] = jnp.zeros_like(acc_ref)

    @pl.when(kj <= qi)
    def _():
        tpos = qi * blk + _iota((blk, blk), 0)
        kpos = kj * blk + _iota((blk, blk), 1)
        mask = kpos <= tpos
        ccol = ccol_ref[0]
        crow = crow_ref[0]
        for hd in range(C_HEADS):
            lane = FOX_GATE_LANE0 + hd
            s = _dot_nt(q_ref[0, hd], k_ref[0, hd]) + ccol[:, lane:lane + 1] - crow[hd:hd + 1, :]
            s = jnp.where(mask, s, NEG_BIG)
            m_old = m_ref[hd]
            m_new = jnp.maximum(m_old, jnp.max(s, axis=-1, keepdims=True))
            p = jnp.where(mask, jnp.exp(s - m_new), 0.0)
            alpha = jnp.exp(m_old - m_new)
            l_ref[hd] = alpha * l_ref[hd] + jnp.sum(p, axis=-1, keepdims=True)
            acc_ref[hd] = alpha * acc_ref[hd] + _dot(p.astype(BF16), v_ref[0, hd])
            m_ref[hd] = m_new

    @pl.when(kj == qi)
    def _():
        for hd in range(C_HEADS):
            o_ref[0, hd] = (acc_ref[hd] / jnp.maximum(l_ref[hd], TINY)).astype(o_ref.dtype)


def _fox(fx, ccol, crow):
    bsz, _, seq, _ = fx.shape
    nb = seq // FOX_BLK
    kv = lambda slot: (lambda b, i, j: (b, slot, jnp.minimum(i, j), 0))
    return pl.pallas_call(
        _fox_body,
        grid=(bsz, nb, nb),
        in_specs=[
            pl.BlockSpec((1, C_HEADS, FOX_BLK, C_DH), lambda b, i, j: (b, 0, i, 0)),
            pl.BlockSpec((1, C_HEADS, FOX_BLK, C_DH), kv(1)),
            pl.BlockSpec((1, C_HEADS, FOX_BLK, C_DH), kv(2)),
            pl.BlockSpec((1, FOX_BLK, GATE_LANES), lambda b, i, j: (b, i, 0)),
            pl.BlockSpec((1, 8, FOX_BLK), lambda b, i, j: (b, 0, jnp.minimum(i, j))),
        ],
        out_specs=pl.BlockSpec((1, C_HEADS, FOX_BLK, C_DH), lambda b, i, j: (b, 0, i, 0)),
        out_shape=jax.ShapeDtypeStruct((bsz, C_HEADS, seq, C_DH), BF16),
        scratch_shapes=[
            pltpu.VMEM((C_HEADS, FOX_BLK, 1), F32),
            pltpu.VMEM((C_HEADS, FOX_BLK, 1), F32),
            pltpu.VMEM((C_HEADS, FOX_BLK, C_DH), F32),
        ],
        compiler_params=_params(("parallel", "parallel", "arbitrary")),
        name="fox_attn",
    )(fx, fx, fx, ccol, crow)


def _cmp_body(x_ref, pos_ref, w1_ref, w2_ref, kg_ref, cos_ref, sl_ref, sh_ref, o_ref):
    n_blk = x_ref.shape[2] // CMP_STRIDE
    hw = B_GROUPS * CMP_HIDDEN
    first = [jnp.zeros((n_blk, hw), F32) for _ in range(2)]
    second = [jnp.zeros((n_blk, hw), F32) for _ in range(2)]
    for l in range(CMP_STRIDE):
        for which in range(2):
            xw = x_ref[which, 0, pl.ds(l, n_blk, stride=CMP_STRIDE), :]
            a = (xw + pos_ref[which, l:l + 1, :]).astype(BF16)
            b = (xw + pos_ref[which, CMP_STRIDE + l:CMP_STRIDE + l + 1, :]).astype(BF16)
            first[which] = first[which] + _dot(a, w1_ref[which, l])
            second[which] = second[which] + _dot(b, w1_ref[which, CMP_STRIDE + l])
    outs = []
    for which in range(2):
        hid = first[which] + pltpu.roll(second[which], n_blk - 1, axis=0)
        hid = 0.5 * hid * (1.0 + jnp.tanh(math.sqrt(2.0 / math.pi) * (hid + 0.044715 * hid * hid * hid)))
        outs.append(_dot(hid.astype(BF16), w2_ref[which]))
    kc = _head_rmsnorm(outs[0], _head_ones(BKV_WIDTH)) * kg_ref[...]
    kc = _rope(kc, cos_ref[...], sl_ref[...], sh_ref[...])
    for g in range(B_GROUPS):
        o_ref[0, g] = kc[:, g * B_DH:(g + 1) * B_DH].astype(BF16)
        o_ref[0, B_GROUPS + g] = outs[1][:, g * B_DH:(g + 1) * B_DH].astype(BF16)


def _compress(cmp4, pos_e, w1_e, w2_e, kg, cos, sl, sh):
    _, bsz, seq, _ = cmp4.shape
    n_blk = seq // CMP_STRIDE
    c2 = lambda b: (0, 0)
    c3 = lambda b: (0, 0, 0)
    c4 = lambda b: (0, 0, 0, 0)
    return pl.pallas_call(
        _cmp_body,
        grid=(bsz,),
        in_specs=[
            pl.BlockSpec((2, 1, seq, BKV_WIDTH), lambda b: (0, b, 0, 0)),
            pl.BlockSpec((2, CMP_BLOCK, BKV_WIDTH), c3),
            pl.BlockSpec((2, CMP_BLOCK, BKV_WIDTH, B_GROUPS * CMP_HIDDEN), c4),
            pl.BlockSpec((2, B_GROUPS * CMP_HIDDEN, BKV_WIDTH), c3),
            pl.BlockSpec((1, BKV_WIDTH), c2),
            pl.BlockSpec((n_blk, BKV_WIDTH), c2),
            pl.BlockSpec((n_blk, BKV_WIDTH), c2),
            pl.BlockSpec((n_blk, BKV_WIDTH), c2),
        ],
        out_specs=pl.BlockSpec((1, 2 * B_GROUPS, n_blk, B_DH), lambda b: (b, 0, 0, 0)),
        out_shape=jax.ShapeDtypeStruct((bsz, 2 * B_GROUPS, n_blk, B_DH), BF16),
        compiler_params=_params(("parallel",)),
        name="nsa_compress",
    )(cmp4, pos_e, w1_e, w2_e, kg, cos, sl, sh)


NSA_Q = 128
NSA_KB = 256


def _softmax_step(s, mask, v, carry):
    m_old, l_old, acc = carry
    s = jnp.where(mask, s, NEG_BIG)
    m_new = jnp.maximum(m_old, jnp.max(s, axis=-1, keepdims=True))
    p = jnp.where(mask, jnp.exp(s - m_new), 0.0)
    alpha = jnp.exp(m_old - m_new)
    return (m_new, alpha * l_old + jnp.sum(p, axis=-1, keepdims=True),
            alpha * acc + _dot(p.astype(BF16), v))


def _nsa_body(q_ref, kcv_ref, kv_ref, gt_ref, o_ref):
    tq = NSA_Q
    rows = B_HPG * tq
    n_cmp = kcv_ref.shape[2]
    n_sel = kv_ref.shape[2] // SEL_BLOCK
    q0 = pl.program_id(1) * tq
    t_row = q0 + _iota((rows, 1), 0) % tq
    gates = _sigmoid(gt_ref[0])

    cn = _iota((n_sel, n_cmp), 1)
    cj = _iota((n_sel, n_cmp), 0)
    overlap_t = jnp.where((cn * CMP_STRIDE <= cj * SEL_BLOCK + SEL_BLOCK - 1)
                          & (cn * CMP_STRIDE + CMP_BLOCK - 1 >= cj * SEL_BLOCK), 1.0, 0.0).astype(BF16)
    eye_q = jnp.where(_iota((tq, tq), 0) == _iota((tq, tq), 1), 1.0, 0.0).astype(BF16)
    jt = _iota((n_sel, tq), 0)
    tt = q0 + _iota((n_sel, tq), 1)
    cur = tt // SEL_BLOCK
    cmp_end = _iota((rows, n_cmp), 1) * CMP_STRIDE + CMP_BLOCK - 1
    lane_kb = _iota((rows, NSA_KB), 1)
    exp_r = _iota((n_sel, NSA_KB), 0)
    exp_c = _iota((n_sel, NSA_KB), 1) // SEL_BLOCK

    for g in range(B_GROUPS):
        q4 = q_ref[0, g * B_HPG:(g + 1) * B_HPG].reshape(rows, B_DH)

        mask_c = cmp_end <= t_row
        s_c = jnp.where(mask_c, _dot_nt(q4, kcv_ref[0, g]), NEG_BIG)
        m_c = jnp.max(s_c, axis=-1, keepdims=True)
        p_c = jnp.where(mask_c, jnp.exp(s_c - m_c), 0.0)
        p_c = p_c / jnp.maximum(jnp.sum(p_c, axis=-1, keepdims=True), TINY)
        o_c = _dot(p_c.astype(BF16), kcv_ref[0, B_GROUPS + g])

        p_sum = p_c[0:tq]
        for hd in range(1, B_HPG):
            p_sum = p_sum + p_c[hd * tq:(hd + 1) * tq]
        p_hi = p_sum.astype(BF16)
        p_lo = (p_sum - p_hi.astype(F32)).astype(BF16)
        imp = _dot_nt(overlap_t, p_hi) + _dot_nt(overlap_t, p_lo)
        imp = jnp.where((jt == 0) | (jt == cur) | (jt == cur - 1), POS_BIG, imp)
        imp = jnp.where(jt * SEL_BLOCK > tt, NEG_BIG, imp)
        rank = jnp.zeros((n_sel, tq), F32)
        for j2 in range(n_sel):
            other = imp[j2:j2 + 1, :]
            beats = (other > imp) | ((other == imp) & (j2 < jt))
            rank = rank + jnp.where(beats, 1.0, 0.0)
        sel_t = jnp.where((rank < SEL_TOPK) & (imp > NEG_BIG * 0.5), 1.0, 0.0).astype(BF16)
        sel = _dot_nt(eye_q, sel_t).astype(BF16)

        init = (jnp.full((rows, 1), NEG_BIG, F32), jnp.zeros((rows, 1), F32), jnp.zeros((rows, B_DH), F32))

        def sel_step(kb, carry):
            st = pl.multiple_of(kb * NSA_KB, NSA_KB)
            kblk = kv_ref[0, g, pl.ds(st, NSA_KB), :]
            vblk = kv_ref[0, B_GROUPS + g, pl.ds(st, NSA_KB), :]
            expand = jnp.where(exp_r == kb * (NSA_KB // SEL_BLOCK) + exp_c, 1.0, 0.0).astype(BF16)
            chosen = _dot(sel, expand)
            chosen = jnp.concatenate([chosen] * B_HPG, axis=0)
            mask = (chosen > 0.5) & (kb * NSA_KB + lane_kb <= t_row)
            return _softmax_step(_dot_nt(q4, kblk), mask, vblk, carry)

        n_kb = (q0 + tq - 1) // NSA_KB + 1
        _, l_s, acc_s = lax.fori_loop(0, n_kb, sel_step, init)
        o_s = acc_s / jnp.maximum(l_s, TINY)

        def win_step(kb, carry):
            st = pl.multiple_of(kb * NSA_KB, NSA_KB)
            kblk = kv_ref[0, 2 * B_GROUPS + g, pl.ds(st, NSA_KB), :]
            vblk = kv_ref[0, 3 * B_GROUPS + g, pl.ds(st, NSA_KB), :]
            kpos = kb * NSA_KB + lane_kb
            mask = (kpos <= t_row) & (kpos > t_row - WINDOW)
            return _softmax_step(_dot_nt(q4, kblk), mask, vblk, carry)

        kb_lo = jnp.maximum(q0 - (WINDOW - 1), 0) // NSA_KB
        _, l_w, acc_w = lax.fori_loop(kb_lo, n_kb, win_step, init)
        o_w = acc_w / jnp.maximum(l_w, TINY)

        def gate_col(branch):
            base = branch * B_HEADS + g * B_HPG
            return jnp.concatenate([gates[:, base + hd:base + hd + 1] for hd in range(B_HPG)], axis=0)

        o = gate_col(0) * o_c + gate_col(1) * o_s + gate_col(2) * o_w
        for hd in range(B_HPG):
            o_ref[0, g * B_HPG + hd] = o[hd * tq:(hd + 1) * tq].astype(o_ref.dtype)


def _nsa(nq, kcv, nkv, gates3):
    bsz, _, seq, _ = nq.shape
    n_cmp = kcv.shape[2]
    return pl.pallas_call(
        _nsa_body,
        grid=(bsz, seq // NSA_Q),
        in_specs=[
            pl.BlockSpec((1, B_HEADS, NSA_Q, B_DH), lambda b, i: (b, 0, i, 0)),
            pl.BlockSpec((1, 2 * B_GROUPS, n_cmp, B_DH), lambda b, i: (b, 0, 0, 0)),
            pl.BlockSpec((1, 4 * B_GROUPS, seq, B_DH), lambda b, i: (b, 0, 0, 0)),
            pl.BlockSpec((1, NSA_Q, GATE_LANES), lambda b, i: (b, i, 0)),
        ],
        out_specs=pl.BlockSpec((1, B_HEADS, NSA_Q, B_DH), lambda b, i: (b, 0, i, 0)),
        out_shape=jax.ShapeDtypeStruct((bsz, B_HEADS, seq, B_DH), BF16),
        compiler_params=_params(("parallel", "arbitrary")),
        name="nsa_attn",
    )(nq, kcv, nkv, gates3)


def _outproj_body(x_ref, oa_ref, ob_ref, oc_ref, w_ref, o_ref):
    acc = x_ref[...] + _dot(oa_ref[...], w_ref[0:A_WIDTH, :])
    for hd in range(B_HEADS):
        lo = A_WIDTH + hd * B_DH
        acc = acc + _dot(ob_ref[0, hd], w_ref[lo:lo + B_DH, :])
    for hd in range(C_HEADS):
        lo = A_WIDTH + BQ_WIDTH + hd * C_DH
        acc = acc + _dot(oc_ref[0, hd], w_ref[lo:lo + C_DH, :])
    o_ref[...] = acc


def _outproj(x2d, oa2d, ob, oc, w_o, bsz, seq, tb=512):
    n = bsz * seq
    nsb = seq // tb
    row = lambda i: (i, 0)
    heads = lambda i: (i // nsb, 0, i % nsb, 0)
    return pl.pallas_call(
        _outproj_body,
        grid=(n // tb,),
        in_specs=[
            pl.BlockSpec((tb, D_MODEL), row),
            pl.BlockSpec((tb, A_WIDTH), row),
            pl.BlockSpec((1, B_HEADS, tb, B_DH), heads),
            pl.BlockSpec((1, C_HEADS, tb, C_DH), heads),
            pl.BlockSpec((D_MODEL, D_MODEL), lambda i: (0, 0)),
        ],
        out_specs=pl.BlockSpec((tb, D_MODEL), row),
        out_shape=jax.ShapeDtypeStruct((n, D_MODEL), F32),
        compiler_params=_params(("parallel",)),
        name="outproj",
    )(x2d, oa2d, ob, oc, w_o)


FFN_ROWS = 1024
FFN_COLS = 1024


def _ffn_body(x_ref, g_ref, wu_ref, wd_ref, o_ref, h_ref, acc_ref):
    j = pl.program_id(1)

    @pl.when(j == 0)
    def _():
        x = x_ref[...]
        var = jnp.mean(x * x, axis=-1, keepdims=True)
        h_ref[...] = (x * lax.rsqrt(var + EPS) * g_ref[...]).astype(BF16)
        acc_ref[...] = x

    u = jnp.maximum(_dot(h_ref[...], wu_ref[...]), 0.0)
    acc_ref[...] += _dot((u * u).astype(BF16), wd_ref[...])

    @pl.when(j == pl.num_programs(1) - 1)
    def _():
        o_ref[...] = acc_ref[...]


def _ffn(x2d, g, w_up, w_down):
    n = x2d.shape[0]
    return pl.pallas_call(
        _ffn_body,
        grid=(n // FFN_ROWS, D_FF // FFN_COLS),
        in_specs=[
            pl.BlockSpec((FFN_ROWS, D_MODEL), lambda i, j: (i, 0)),
            pl.BlockSpec((1, D_MODEL), lambda i, j: (0, 0)),
            pl.BlockSpec((D_MODEL, FFN_COLS), lambda i, j: (0, j)),
            pl.BlockSpec((FFN_COLS, D_MODEL), lambda i, j: (j, 0)),
        ],
        out_specs=pl.BlockSpec((FFN_ROWS, D_MODEL), lambda i, j: (i, 0)),
        out_shape=jax.ShapeDtypeStruct((n, D_MODEL), F32),
        scratch_shapes=[pltpu.VMEM((FFN_ROWS, D_MODEL), BF16), pltpu.VMEM((FFN_ROWS, D_MODEL), F32)],
        compiler_params=_params(("parallel", "arbitrary")),
        name="ffn",
    )(x2d, g, w_up, w_down)


def _rope_tables(pos, n_heads):
    inv_freq = jnp.power(jnp.float32(ROPE_THETA), -jnp.arange(0, ROT_DIM, 2, dtype=F32) / ROT_DIM)
    ang = pos.astype(F32)[:, None] * inv_freq[None, :]
    cos, sin = jnp.cos(ang), jnp.sin(ang)
    n = pos.shape[0]
    pad = jnp.zeros((n, HEAD_DIM - ROT_DIM), F32)
    zero = jnp.zeros((n, ROT_HALF), F32)
    c = jnp.concatenate([cos, cos, pad + 1.0], axis=1)
    s_lo = jnp.concatenate([zero, sin, pad], axis=1)
    s_hi = jnp.concatenate([-sin, zero, pad], axis=1)
    return tuple(jnp.tile(t, (1, n_heads)) for t in (c, s_lo, s_hi))


def _pack_w_in(w):
    n_main = OFF_GATE
    gate_cols = jnp.concatenate(
        [w[:, n_main:n_main + N_GATES], w[:, -C_HEADS:],
         jnp.zeros((D_MODEL, GATE_LANES - N_GATES - C_HEADS), w.dtype)], axis=1)
    return jnp.concatenate([w[:, :n_main], gate_cols, w[:, n_main + N_GATES:-C_HEADS]], axis=1).astype(BF16)


def _expand_groups(w):
    z = jnp.zeros_like(w)
    top = jnp.concatenate([w, z], axis=-1)
    bot = jnp.concatenate([z, w], axis=-1)
    return jnp.concatenate([top, bot], axis=-2)


def kernel(x, norm1_g, w_in, hgrn_lb_logits, hgrn_onorm_g, nsa_qn_g, nsa_kn_g, nsa_cmp_pos, nsa_cmp_w1,
           nsa_cmp_w2, fox_qn_g, fox_kn_g, fox_fb, w_o, norm2_g, w_up, w_down):
    bsz, seq, _ = x.shape
    n = bsz * seq
    n_blk = seq // CMP_STRIDE
    cos, s_lo, s_hi = _rope_tables(jnp.arange(seq), B_HEADS)
    ccos, cs_lo, cs_hi = _rope_tables(jnp.arange(n_blk) * CMP_STRIDE + CMP_BLOCK - 1, B_GROUPS)

    x2d = x.reshape(n, D_MODEL)
    for layer in range(DEPTH):
        row = lambda v, reps: jnp.tile(v[layer].astype(F32), reps)[None, :]
        w_all = _pack_w_in(w_in[layer])
        hg, nq, nkv, cmp2d, gates, fx = _inproj(
            x2d, norm1_g[layer][None, :], w_all, cos, s_lo, s_hi,
            row(nsa_qn_g, B_HEADS), row(nsa_kn_g, B_GROUPS), row(fox_qn_g, C_HEADS), row(fox_kn_g, C_HEADS),
            bsz, seq)

        o_a = _hgrn(hg.reshape(bsz, seq, 4 * A_WIDTH), hgrn_lb_logits.astype(F32),
                    row(hgrn_onorm_g, A_HEADS), layer)

        gates3 = gates.reshape(bsz, seq, GATE_LANES)
        fb_row = jnp.zeros((1, GATE_LANES), F32).at[0, FOX_GATE_LANE0:FOX_GATE_LANE0 + C_HEADS].set(
            fox_fb[layer].astype(F32))
        ccol, crow = _foxc(gates3, fb_row)
        o_c = _fox(fx, ccol, crow)

        w1 = nsa_cmp_w1[layer].reshape(2, CMP_BLOCK, B_DH, CMP_HIDDEN)
        w1_e = _expand_groups(w1).astype(BF16)
        w2_e = _expand_groups(nsa_cmp_w2[layer]).astype(BF16)
        pos_e = jnp.tile(nsa_cmp_pos[layer].astype(F32), (1, 1, B_GROUPS))
        kcv = _compress(cmp2d.reshape(2, bsz, seq, BKV_WIDTH), pos_e, w1_e, w2_e,
                        row(nsa_kn_g, B_GROUPS), ccos, cs_lo, cs_hi)
        o_b = _nsa(nq, kcv, nkv, gates3)

        x2d = _outproj(x2d, o_a.reshape(n, A_WIDTH), o_b, o_c, w_o[layer].astype(BF16), bsz, seq)
        x2d = _ffn(x2d, norm2_g[layer][None, :], w_up[layer].astype(BF16), w_down[layer].astype(BF16))
    return x2d.reshape(bsz, seq, D_MODEL)
```

```python
import functools
import math

import numpy as np
import jax
import jax.numpy as jnp
from jax import lax
from jax.experimental import pallas as pl
from jax.experimental.pallas import tpu as pltpu

F32 = jnp.float32
BF16 = jnp.bfloat16

D_MODEL = 1024
DEPTH = 2
A_HEADS, A_DK, A_DV = 4, 64, 64
B_HEADS, B_GROUPS, B_DH = 8, 2, 64
B_HPG = B_HEADS // B_GROUPS
C_HEADS, C_DH = 4, 64
HEAD_DIM = 64
D_FF = 4 * D_MODEL
ROPE_THETA = 500000.0
ROT_DIM = B_DH // 4
ROT_HALF = ROT_DIM // 2
EPS = 1e-6
NEG_BIG = -1e30
POS_BIG = 1e30
TINY = 1e-30
CMP_BLOCK, CMP_STRIDE, CMP_HIDDEN = 32, 16, 128
SEL_BLOCK, SEL_TOPK, WINDOW = 64, 16, 512

A_WIDTH = A_HEADS * A_DK
BQ_WIDTH = B_HEADS * B_DH
BKV_WIDTH = B_GROUPS * B_DH
C_WIDTH = C_HEADS * C_DH
N_GATES = 3 * B_HEADS
GATE_LANES = 128
FOX_GATE_LANE0 = N_GATES

OFF_A = 0
OFF_BQ = 4 * A_WIDTH
OFF_KC = OFF_BQ + BQ_WIDTH
OFF_VC = OFF_KC + BKV_WIDTH
OFF_KS = OFF_VC + BKV_WIDTH
OFF_VS = OFF_KS + BKV_WIDTH
OFF_KW = OFF_VS + BKV_WIDTH
OFF_VW = OFF_KW + BKV_WIDTH
OFF_GATE = OFF_VW + BKV_WIDTH
OFF_CQ = OFF_GATE + GATE_LANES
OFF_CK = OFF_CQ + C_WIDTH
OFF_CV = OFF_CK + C_WIDTH
PROJ_WIDTH = OFF_CV + C_WIDTH

VMEM_LIMIT = 56 * 1024 * 1024


def _params(sem):
    return pltpu.CompilerParams(dimension_semantics=sem, vmem_limit_bytes=VMEM_LIMIT)


def _dot(a, b):
    return jnp.dot(a, b, preferred_element_type=F32)


def _dot_nt(a, b):
    return lax.dot_general(a, b, (((1,), (1,)), ((), ())), preferred_element_type=F32)


def _dot_tn(a, b):
    return lax.dot_general(a, b, (((0,), (0,)), ((), ())), preferred_element_type=F32)


def _split3(x):
    a = x.astype(BF16)
    r = x - a.astype(F32)
    b = r.astype(BF16)
    c = (r - b.astype(F32)).astype(BF16)
    return a, b, c


def _dot_exact_rhs(m01, x):
    a, b, c = _split3(x)
    return _dot(m01, a) + _dot(m01, b) + _dot(m01, c)


def _iota(shape, dim):
    return lax.broadcasted_iota(jnp.int32, shape, dim)


def _head_ones(width):
    r = _iota((width, width), 0) // HEAD_DIM
    c = _iota((width, width), 1) // HEAD_DIM
    return jnp.where(r == c, 1.0, 0.0).astype(BF16)


def _head_rmsnorm(x, head_ones):
    x2 = x * x
    hi = x2.astype(BF16)
    lo = (x2 - hi.astype(F32)).astype(BF16)
    ms = (_dot(hi, head_ones) + _dot(lo, head_ones)) * (1.0 / HEAD_DIM)
    return x * lax.rsqrt(ms + EPS)


def _rope(x, cos, sin_lo, sin_hi):
    n = x.shape[-1]
    return (x * cos + pltpu.roll(x, ROT_HALF, axis=1) * sin_lo
            + pltpu.roll(x, n - ROT_HALF, axis=1) * sin_hi)


def _sigmoid(x):
    return 1.0 / (1.0 + jnp.exp(-x))


TOFF_Q = 0
TOFF_V = BQ_WIDTH
TOFF_GATE = TOFF_V + 2 * BKV_WIDTH
GATE_ROWS = 32
TOFF_FQ = TOFF_GATE + GATE_ROWS
TOFF_FV = TOFF_FQ + C_WIDTH
T_ROWS = TOFF_FV + C_WIDTH


def _inproj_body(seq_blocks, x_ref, g_ref, w_ref, wt_ref, cos_ref, sl_ref, sh_ref, cost_ref, sint_ref, qg_ref,
                 kg_ref, fqg_ref, fkg_ref, hg_ref, qt_ref, ka_ref, vt_ref, cmp_ref, gt_ref, gtt_ref,
                 fqt_ref, fk_ref, fvt_ref):
    tb = x_ref.shape[0]
    x = x_ref[...]
    var = jnp.mean(x * x, axis=-1, keepdims=True)
    h = (x * lax.rsqrt(var + EPS) * g_ref[...]).astype(BF16)

    def proj(lo, hi):
        return _dot(h, w_ref[:, lo:hi])

    def proj_t(lo, hi):
        return _dot_nt(wt_ref[lo:hi, :], h)

    hg_ref[...] = proj(OFF_A, OFF_BQ)

    q = proj_t(TOFF_Q, TOFF_V).reshape(B_HEADS, B_DH, tb)
    q = q * lax.rsqrt(jnp.mean(q * q, axis=1, keepdims=True) + EPS) * qg_ref[...]
    cost, sint = cost_ref[...], sint_ref[...]
    x1, x2 = q[:, 0:ROT_HALF], q[:, ROT_HALF:ROT_DIM]
    q = jnp.concatenate([x1 * cost - x2 * sint, x2 * cost + x1 * sint, q[:, ROT_DIM:]], axis=1)
    qt_ref[0] = (q * (B_DH ** -0.5)).reshape(BQ_WIDTH, tb).astype(BF16)

    vt_ref[0] = proj_t(TOFF_V, TOFF_GATE).astype(BF16)
    gtt_ref[0] = proj_t(TOFF_GATE, TOFF_FQ)

    cmp_ref[0] = proj(OFF_KC, OFF_VC)
    cmp_ref[1] = proj(OFF_VC, OFF_KS)

    cos, sl, sh = cos_ref[...], sl_ref[...], sh_ref[...]
    ones_kv = _head_ones(BKV_WIDTH)
    ones_c = _head_ones(C_WIDTH)
    lane = _iota((tb, BKV_WIDTH), 1)
    tok = (pl.program_id(0) % seq_blocks) * tb + _iota((tb, BKV_WIDTH), 0)
    onehot = jnp.where(lane == B_DH + tok // SEL_BLOCK, 1.0, 0.0)
    for slot, off in ((0, OFF_KS), (1, OFF_KW)):
        t = _rope(_head_rmsnorm(proj(off, off + BKV_WIDTH), ones_kv) * kg_ref[...], cos, sl, sh)
        for g in range(B_GROUPS):
            low = t if g == 0 else pltpu.roll(t, B_DH, axis=1)
            ka_ref[0, slot * B_GROUPS + g] = jnp.where(lane < B_DH, low, onehot).astype(BF16)

    gt_ref[...] = proj(OFF_GATE, OFF_CQ)

    fq = proj_t(TOFF_FQ, TOFF_FV).reshape(C_HEADS, C_DH, tb)
    fq = fq * lax.rsqrt(jnp.mean(fq * fq, axis=1, keepdims=True) + EPS) * fqg_ref[...] * (C_DH ** -0.5)
    fqt_ref[0] = fq.reshape(C_WIDTH, tb).astype(BF16)
    fvt_ref[0] = proj_t(TOFF_FV, T_ROWS).astype(BF16)
    fk = _head_rmsnorm(proj(OFF_CK, OFF_CV), ones_c) * fkg_ref[...]
    for hd in range(C_HEADS):
        slab = fk[:, (hd // 2) * BKV_WIDTH:(hd // 2 + 1) * BKV_WIDTH]
        low = slab if hd % 2 == 0 else pltpu.roll(slab, C_DH, axis=1)
        fk_ref[0, hd] = jnp.where(lane < C_DH, low, 0.0).astype(BF16)


def _inproj(x2d, g, w, wt, cos, sl, sh, cost, sint, qg, kg, fqg, fkg, bsz, seq, tb=256):
    n = bsz * seq
    nsb = seq // tb
    row = lambda i: (i, 0)
    const = lambda i: (0, 0)
    pos = lambda i: (i % nsb, 0)
    pos_t = lambda i: (0, i % nsb)
    heads = lambda i: (i // nsb, 0, i % nsb, 0)
    tokens_on_lanes = lambda i: (i // nsb, 0, i % nsb)
    return pl.pallas_call(
        functools.partial(_inproj_body, nsb),
        grid=(n // tb,),
        in_specs=[
            pl.BlockSpec((tb, D_MODEL), row),
            pl.BlockSpec((1, D_MODEL), const),
            pl.BlockSpec((D_MODEL, PROJ_WIDTH), const),
            pl.BlockSpec((T_ROWS, D_MODEL), const),
            pl.BlockSpec((tb, BKV_WIDTH), pos),
            pl.BlockSpec((tb, BKV_WIDTH), pos),
            pl.BlockSpec((tb, BKV_WIDTH), pos),
            pl.BlockSpec((ROT_HALF, tb), pos_t),
            pl.BlockSpec((ROT_HALF, tb), pos_t),
            pl.BlockSpec((B_DH, 1), const),
            pl.BlockSpec((1, BKV_WIDTH), const),
            pl.BlockSpec((C_DH, 1), const),
            pl.BlockSpec((1, C_WIDTH), const),
        ],
        out_specs=[
            pl.BlockSpec((tb, 4 * A_WIDTH), row),
            pl.BlockSpec((1, BQ_WIDTH, tb), tokens_on_lanes),
            pl.BlockSpec((1, 2 * B_GROUPS, tb, BKV_WIDTH), heads),
            pl.BlockSpec((1, 2 * BKV_WIDTH, tb), tokens_on_lanes),
            pl.BlockSpec((2, tb, BKV_WIDTH), lambda i: (0, i, 0)),
            pl.BlockSpec((tb, GATE_LANES), row),
            pl.BlockSpec((1, GATE_ROWS, tb), tokens_on_lanes),
            pl.BlockSpec((1, C_WIDTH, tb), tokens_on_lanes),
            pl.BlockSpec((1, C_HEADS, tb, BKV_WIDTH), heads),
            pl.BlockSpec((1, C_WIDTH, tb), tokens_on_lanes),
        ],
        out_shape=[
            jax.ShapeDtypeStruct((n, 4 * A_WIDTH), F32),
            jax.ShapeDtypeStruct((bsz, BQ_WIDTH, seq), BF16),
            jax.ShapeDtypeStruct((bsz, 2 * B_GROUPS, seq, BKV_WIDTH), BF16),
            jax.ShapeDtypeStruct((bsz, 2 * BKV_WIDTH, seq), BF16),
            jax.ShapeDtypeStruct((2, n, BKV_WIDTH), F32),
            jax.ShapeDtypeStruct((n, GATE_LANES), F32),
            jax.ShapeDtypeStruct((bsz, GATE_ROWS, seq), F32),
            jax.ShapeDtypeStruct((bsz, C_WIDTH, seq), BF16),
            jax.ShapeDtypeStruct((bsz, C_HEADS, seq, BKV_WIDTH), BF16),
            jax.ShapeDtypeStruct((bsz, C_WIDTH, seq), BF16),
        ],
        compiler_params=_params(("parallel",)),
        name="inproj",
    )(x2d, g, w, wt, cos, sl, sh, cost, sint, qg, kg, fqg, fkg)


HGRN_SUB = 16
HGRN_ROWS = 128


def _hgrn_body(layer, hg_ref, lbl_ref, og_ref, o_ref, st_ref):
    @pl.when(pl.program_id(1) == 0)
    def _():
        st_ref[...] = jnp.zeros_like(st_ref)

    rows, width = HGRN_ROWS, A_WIDTH
    lbl = lbl_ref[...]
    e = jnp.exp(lbl - jnp.max(lbl, axis=0, keepdims=True))
    p = e / jnp.sum(e, axis=0, keepdims=True)
    lb = p[0:1] * 0.0
    for l in range(1, layer + 1):
        lb = lb + p[l:l + 1]

    q = hg_ref[0, :, 0:width] * (A_DK ** -0.5)
    z = hg_ref[0, :, width:2 * width]
    v = hg_ref[0, :, 2 * width:3 * width]
    go = hg_ref[0, :, 3 * width:4 * width]

    f = lb + (1.0 - lb) * _sigmoid(z)
    logf = jnp.log(jnp.maximum(f, TINY))
    k = (1.0 - lb) * _sigmoid(-z)

    r = _iota((rows, rows), 0)
    c = _iota((rows, rows), 1)
    tri = jnp.where((r // HGRN_SUB == c // HGRN_SUB) & (c <= r), 1.0, 0.0).astype(BF16)
    gl = _dot_exact_rhs(tri, logf)

    ones_h = _head_ones(width)
    tin = _iota((rows, width), 0) % HGRN_SUB

    o_band = jnp.zeros((rows, width), F32)
    for d in range(HGRN_SUB):
        if d == 0:
            ks, gs, vs = k, gl, v
        else:
            ks, gs, vs = (pltpu.roll(a, d, axis=0) for a in (k, gl, v))
        valid = tin >= d
        pr = jnp.where(valid, q * ks * jnp.exp(jnp.minimum(gl - gs, 0.0)), 0.0)
        o_band = o_band + _dot(pr.astype(BF16), ones_h) * vs

    hm = _iota((width, width), 0) // HEAD_DIM == _iota((width, width), 1) // HEAD_DIM
    st = st_ref[...]
    outs = []
    for i in range(rows // HGRN_SUB):
        sl_ = slice(i * HGRN_SUB, (i + 1) * HGRN_SUB)
        gi, qi, ki, vi = gl[sl_], q[sl_], k[sl_], v[sl_]
        outs.append(_dot_nt((qi * jnp.exp(gi)).astype(BF16), st.astype(BF16)))
        glast = gi[HGRN_SUB - 1:HGRN_SUB]
        kdec = ki * jnp.exp(glast - gi)
        upd = _dot_tn(vi.astype(BF16), kdec.astype(BF16))
        st = st * jnp.exp(glast) + jnp.where(hm, upd, 0.0)
    st_ref[...] = st
    o = jnp.concatenate(outs, axis=0) + o_band

    o = _head_rmsnorm(o, ones_h) * og_ref[...]
    gate = go * _sigmoid(go)
    o_ref[0] = (o * gate).astype(o_ref.dtype)


def _hgrn(hg3, lb_logits, og, layer):
    bsz, seq, _ = hg3.shape
    return pl.pallas_call(
        functools.partial(_hgrn_body, layer),
        grid=(bsz, seq // HGRN_ROWS),
        in_specs=[
            pl.BlockSpec((1, HGRN_ROWS, 4 * A_WIDTH), lambda b, i: (b, i, 0)),
            pl.BlockSpec((DEPTH, A_WIDTH), lambda b, i: (0, 0)),
            pl.BlockSpec((1, A_WIDTH), lambda b, i: (0, 0)),
        ],
        out_specs=pl.BlockSpec((1, HGRN_ROWS, A_WIDTH), lambda b, i: (b, i, 0)),
        out_shape=jax.ShapeDtypeStruct((bsz, seq, A_WIDTH), BF16),
        scratch_shapes=[pltpu.VMEM((A_WIDTH, A_WIDTH), F32)],
        compiler_params=_params(("parallel", "arbitrary")),
        name="hgrn2",
    )(hg3, lb_logits, og)


FOXC_ROWS = 512


FOX_AUG = 16


def _foxc_body(gt_ref, fb_ref, ck_ref, cq_ref, carry_ref):
    @pl.when(pl.program_id(1) == 0)
    def _():
        carry_ref[...] = jnp.zeros_like(carry_ref)

    rows = FOXC_ROWS
    y = gt_ref[0] + fb_ref[...]
    lf = jnp.minimum(y, 0.0) - jnp.log(1.0 + jnp.exp(-jnp.abs(y)))
    tri = jnp.where(_iota((rows, rows), 1) <= _iota((rows, rows), 0), 1.0, 0.0).astype(BF16)
    cc = _dot_exact_rhs(tri, lf) + carry_ref[...]
    carry_ref[...] = cc[rows - 1:rows]
    parts = jnp.concatenate(_split3(cc), axis=1)

    pr = _iota((3 * GATE_LANES, BKV_WIDTH), 0)
    pc = _iota((3 * GATE_LANES, BKV_WIDTH), 1)
    sr = _iota((FOX_AUG, 3 * GATE_LANES), 0)
    sc = _iota((FOX_AUG, 3 * GATE_LANES), 1)
    k_lane = _iota((rows, BKV_WIDTH), 1)
    ones_k = jnp.where((k_lane >= C_DH) & (k_lane < C_DH + 3), 1.0, 0.0)
    q_row = _iota((FOX_AUG, rows), 0)
    ones_q = jnp.where((q_row >= 3) & (q_row < 6), 1.0, 0.0)
    for hd in range(C_HEADS):
        src = FOX_GATE_LANE0 + hd
        place_k = jnp.where((pr % GATE_LANES == src) & (pc == C_DH + 3 + pr // GATE_LANES), -1.0, 0.0).astype(BF16)
        place_q = jnp.where((sc % GATE_LANES == src) & (sr == sc // GATE_LANES), 1.0, 0.0).astype(BF16)
        ck_ref[0, hd] = (_dot(parts, place_k) + ones_k).astype(BF16)
        cq_ref[0, hd] = (_dot_nt(place_q, parts) + ones_q).astype(BF16)


def _foxc(gates3, fb_row):
    bsz, seq, _ = gates3.shape
    return pl.pallas_call(
        _foxc_body,
        grid=(bsz, seq // FOXC_ROWS),
        in_specs=[
            pl.BlockSpec((1, FOXC_ROWS, GATE_LANES), lambda b, i: (b, i, 0)),
            pl.BlockSpec((1, GATE_LANES), lambda b, i: (0, 0)),
        ],
        out_specs=[
            pl.BlockSpec((1, C_HEADS, FOXC_ROWS, BKV_WIDTH), lambda b, i: (b, 0, i, 0)),
            pl.BlockSpec((1, C_HEADS, FOX_AUG, FOXC_ROWS), lambda b, i: (b, 0, 0, i)),
        ],
        out_shape=[
            jax.ShapeDtypeStruct((bsz, C_HEADS, seq, BKV_WIDTH), BF16),
            jax.ShapeDtypeStruct((bsz, C_HEADS, FOX_AUG, seq), BF16),
        ],
        scratch_shapes=[pltpu.VMEM((1, GATE_LANES), F32)],
        compiler_params=_params(("parallel", "arbitrary")),
        name="fox_cumgate",
    )(gates3, fb_row)


FOX_BLK = 512


def _fox_body(qt_ref, cq_ref, k_ref, ck_ref, vt_ref, o_ref, m_ref, l_ref, acc_ref):
    qi, kj = pl.program_id(1), pl.program_id(2)
    blk = FOX_BLK

    @pl.when(kj == 0)
    def _():
        m_ref[...] = jnp.full_like(m_ref, NEG_BIG)
        l_ref[...] = jnp.zeros_like(l_ref)
        acc_ref[...] = jnp.zeros_like(acc_ref)

    def update(diagonal):
        pad = jnp.zeros((BKV_WIDTH - C_DH - FOX_AUG, blk), BF16)
        for hd in range(C_HEADS):
            rows = slice(hd * C_DH, (hd + 1) * C_DH)
            q_aug = jnp.concatenate([qt_ref[0, rows, :], cq_ref[0, hd], pad], axis=0)
            s = _dot(k_ref[0, hd] + ck_ref[0, hd], q_aug)
            if diagonal:
                s = jnp.where(_iota((blk, blk), 0) <= _iota((blk, blk), 1), s, NEG_BIG)
            m_ref[hd], l_ref[hd], acc_ref[hd] = _softmax_step_t(
                s, vt_ref[0, rows, :], (m_ref[hd], l_ref[hd], acc_ref[hd]))

    @pl.when(kj < qi)
    def _():
        update(False)

    @pl.when(kj == qi)
    def _():
        update(True)
        for hd in range(C_HEADS):
            rows = slice(hd * C_DH, (hd + 1) * C_DH)
            o_ref[0, rows, :] = (acc_ref[hd] / jnp.maximum(l_ref[hd], TINY)).astype(o_ref.dtype)


def _fox(fqt, cq, fk, ck, fvt):
    bsz, _, seq = fqt.shape
    nb = seq // FOX_BLK
    q_lanes = lambda b, i, j: (b, 0, i)
    k_rows = lambda b, i, j: (b, 0, jnp.minimum(i, j), 0)
    return pl.pallas_call(
        _fox_body,
        grid=(bsz, nb, nb),
        in_specs=[
            pl.BlockSpec((1, C_WIDTH, FOX_BLK), q_lanes),
            pl.BlockSpec((1, C_HEADS, FOX_AUG, FOX_BLK), lambda b, i, j: (b, 0, 0, i)),
            pl.BlockSpec((1, C_HEADS, FOX_BLK, BKV_WIDTH), k_rows),
            pl.BlockSpec((1, C_HEADS, FOX_BLK, BKV_WIDTH), k_rows),
            pl.BlockSpec((1, C_WIDTH, FOX_BLK), lambda b, i, j: (b, 0, jnp.minimum(i, j))),
        ],
        out_specs=pl.BlockSpec((1, C_WIDTH, FOX_BLK), q_lanes),
        out_shape=jax.ShapeDtypeStruct((bsz, C_WIDTH, seq), BF16),
        scratch_shapes=[
            pltpu.VMEM((C_HEADS, 1, FOX_BLK), F32),
            pltpu.VMEM((C_HEADS, 1, FOX_BLK), F32),
            pltpu.VMEM((C_HEADS, C_DH, FOX_BLK), F32),
        ],
        compiler_params=_params(("parallel", "parallel", "arbitrary")),
        name="fox_attn",
    )(fqt, cq, fk, ck, fvt)


def _cmp_body(x_ref, pos_ref, w1_ref, w2k_ref, w2vt_ref, kg_ref, cos_ref, sl_ref, sh_ref, kc_ref, vct_ref):
    n_blk = x_ref.shape[2] // CMP_STRIDE
    hw = B_GROUPS * CMP_HIDDEN
    first = [jnp.zeros((n_blk, hw), F32) for _ in range(2)]
    second = [jnp.zeros((n_blk, hw), F32) for _ in range(2)]
    for l in range(CMP_STRIDE):
        for which in range(2):
            xw = x_ref[which, 0, pl.ds(l, n_blk, stride=CMP_STRIDE), :]
            a = (xw + pos_ref[which, l:l + 1, :]).astype(BF16)
            b = (xw + pos_ref[which, CMP_STRIDE + l:CMP_STRIDE + l + 1, :]).astype(BF16)
            first[which] = first[which] + _dot(a, w1_ref[which, l])
            second[which] = second[which] + _dot(b, w1_ref[which, CMP_STRIDE + l])
    hids = []
    for which in range(2):
        hid = first[which] + pltpu.roll(second[which], n_blk - 1, axis=0)
        hid = 0.5 * hid * (1.0 + jnp.tanh(math.sqrt(2.0 / math.pi) * (hid + 0.044715 * hid * hid * hid)))
        hids.append(hid.astype(BF16))
    kc = _dot(hids[0], w2k_ref[...])
    kc = _head_rmsnorm(kc, _head_ones(BKV_WIDTH)) * kg_ref[...]
    kc = _rope(kc, cos_ref[...], sl_ref[...], sh_ref[...])
    vct = _dot_nt(w2vt_ref[...], hids[1])
    for g in range(B_GROUPS):
        kc_ref[0, g] = kc[:, g * B_DH:(g + 1) * B_DH].astype(BF16)
        vct_ref[0, g] = vct[g * B_DH:(g + 1) * B_DH].astype(BF16)


def _compress(cmp4, pos_e, w1_e, w2k_e, w2vt_e, kg, cos, sl, sh):
    _, bsz, seq, _ = cmp4.shape
    n_blk = seq // CMP_STRIDE
    c2 = lambda b: (0, 0)
    c3 = lambda b: (0, 0, 0)
    c4 = lambda b: (0, 0, 0, 0)
    return pl.pallas_call(
        _cmp_body,
        grid=(bsz,),
        in_specs=[
            pl.BlockSpec((2, 1, seq, BKV_WIDTH), lambda b: (0, b, 0, 0)),
            pl.BlockSpec((2, CMP_BLOCK, BKV_WIDTH), c3),
            pl.BlockSpec((2, CMP_BLOCK, BKV_WIDTH, B_GROUPS * CMP_HIDDEN), c4),
            pl.BlockSpec((B_GROUPS * CMP_HIDDEN, BKV_WIDTH), c2),
            pl.BlockSpec((BKV_WIDTH, B_GROUPS * CMP_HIDDEN), c2),
            pl.BlockSpec((1, BKV_WIDTH), c2),
            pl.BlockSpec((n_blk, BKV_WIDTH), c2),
            pl.BlockSpec((n_blk, BKV_WIDTH), c2),
            pl.BlockSpec((n_blk, BKV_WIDTH), c2),
        ],
        out_specs=[
            pl.BlockSpec((1, B_GROUPS, n_blk, B_DH), lambda b: (b, 0, 0, 0)),
            pl.BlockSpec((1, B_GROUPS, B_DH, n_blk), lambda b: (b, 0, 0, 0)),
        ],
        out_shape=[
            jax.ShapeDtypeStruct((bsz, B_GROUPS, n_blk, B_DH), BF16),
            jax.ShapeDtypeStruct((bsz, B_GROUPS, B_DH, n_blk), BF16),
        ],
        compiler_params=_params(("parallel",)),
        name="nsa_compress",
    )(cmp4, pos_e, w1_e, w2k_e, w2vt_e, kg, cos, sl, sh)


NSA_Q = 256
NSA_KB = 256
assert NSA_KB % NSA_Q == 0, "a query block must sit inside one key block (single diagonal step)"


def _softmax_step_t(s, vt, carry):
    m_old, l_old, acc = carry
    m_new = jnp.maximum(m_old, jnp.max(s, axis=0, keepdims=True))
    p = jnp.exp(s - m_new)
    alpha = jnp.exp(m_old - m_new)
    return (m_new, alpha * l_old + jnp.sum(p, axis=0, keepdims=True),
            alpha * acc + _dot(vt, p.astype(BF16)))


def _nsa_body(qt_ref, kc_ref, vct_ref, ka_ref, vt_ref, gt_ref, o_ref):
    tq = NSA_Q
    lanes = B_HPG * tq
    n_cmp = kc_ref.shape[2]
    n_sel = ka_ref.shape[2] // SEL_BLOCK
    q0 = pl.program_id(1) * tq
    t_lane = q0 + _iota((1, lanes), 1) % tq
    gates = _sigmoid(gt_ref[0])

    cn = _iota((n_sel, n_cmp), 1)
    cj = _iota((n_sel, n_cmp), 0)
    overlap_t = jnp.where((cn * CMP_STRIDE <= cj * SEL_BLOCK + SEL_BLOCK - 1)
                          & (cn * CMP_STRIDE + CMP_BLOCK - 1 >= cj * SEL_BLOCK), 1.0, 0.0).astype(BF16)
    jt = _iota((n_sel, tq), 0)
    tt = q0 + _iota((n_sel, tq), 1)
    cur = tt // SEL_BLOCK
    cmp_end = _iota((n_cmp, lanes), 0) * CMP_STRIDE + CMP_BLOCK - 1
    key_kb = _iota((NSA_KB, lanes), 0)
    kb_diag = q0 // NSA_KB
    zeros_aug = jnp.zeros((B_DH, lanes), BF16)

    head_rows = [[slice((g * B_HPG + hd) * B_DH, (g * B_HPG + hd + 1) * B_DH) for hd in range(B_HPG)]
                 for g in range(B_GROUPS)]
    o_cmp, q_sel, q_win = [], [], []
    for g in range(B_GROUPS):
        qt = jnp.concatenate([qt_ref[0, r, :] for r in head_rows[g]], axis=1)

        mask_c = cmp_end <= t_lane
        s_c = jnp.where(mask_c, _dot(kc_ref[0, g], qt), NEG_BIG)
        m_c = jnp.max(s_c, axis=0, keepdims=True)
        p_c = jnp.where(mask_c, jnp.exp(s_c - m_c), 0.0)
        p_c = p_c / jnp.maximum(jnp.sum(p_c, axis=0, keepdims=True), TINY)
        o_cmp.append(_dot(vct_ref[0, g], p_c.astype(BF16)))

        p_sum = p_c[:, 0:tq]
        for hd in range(1, B_HPG):
            p_sum = p_sum + p_c[:, hd * tq:(hd + 1) * tq]
        p_hi = p_sum.astype(BF16)
        p_lo = (p_sum - p_hi.astype(F32)).astype(BF16)
        imp = _dot(overlap_t, p_hi) + _dot(overlap_t, p_lo)
        imp = jnp.where((jt == 0) | (jt == cur) | (jt == cur - 1), POS_BIG, imp)
        imp = jnp.where(jt * SEL_BLOCK > tt, NEG_BIG, imp)
        rank = jnp.zeros((n_sel, tq), F32)
        for j2 in range(n_sel):
            other = imp[j2:j2 + 1, :]
            beats = (other > imp) | ((other == imp) & (j2 < jt))
            rank = rank + jnp.where(beats, 1.0, 0.0)
        bias = jnp.where((rank < SEL_TOPK) & (imp > NEG_BIG * 0.5), 0.0, NEG_BIG).astype(BF16)
        if n_sel < BKV_WIDTH - B_DH:
            bias = jnp.concatenate([bias, jnp.zeros((BKV_WIDTH - B_DH - n_sel, tq), BF16)], axis=0)
        q_sel.append(jnp.concatenate([qt, jnp.concatenate([bias] * B_HPG, axis=1)], axis=0))
        q_win.append(jnp.concatenate([qt, zeros_aug], axis=0))

    init = (jnp.full((1, lanes), NEG_BIG, F32), jnp.zeros((1, lanes), F32), jnp.zeros((B_DH, lanes), F32))
    init = (init,) * B_GROUPS

    def scores(slot, g, kb, q_aug):
        st = pl.multiple_of(kb * NSA_KB, NSA_KB)
        s = _dot(ka_ref[0, slot * B_GROUPS + g, pl.ds(st, NSA_KB), :], q_aug)
        vt = vt_ref[0, (slot * B_GROUPS + g) * B_DH:(slot * B_GROUPS + g + 1) * B_DH, pl.ds(st, NSA_KB)]
        return s, vt, kb * NSA_KB + key_kb

    def sel_step(kb, carry):
        out = []
        for g in range(B_GROUPS):
            s, vt, _ = scores(0, g, kb, q_sel[g])
            out.append(_softmax_step_t(s, vt, carry[g]))
        return tuple(out)

    carry = lax.fori_loop(0, kb_diag, sel_step, init)
    o_sel = []
    for g in range(B_GROUPS):
        s, vt, kpos = scores(0, g, kb_diag, q_sel[g])
        _, l_s, acc_s = _softmax_step_t(jnp.where(kpos <= t_lane, s, NEG_BIG), vt, carry[g])
        o_sel.append(acc_s / jnp.maximum(l_s, TINY))

    def win_step(kb, carry):
        out = []
        for g in range(B_GROUPS):
            s, vt, kpos = scores(1, g, kb, q_win[g])
            mask = (kpos <= t_lane) & (kpos > t_lane - WINDOW)
            out.append(_softmax_step_t(jnp.where(mask, s, NEG_BIG), vt, carry[g]))
        return tuple(out)

    kb_lo = jnp.maximum(q0 - (WINDOW - 1), 0) // NSA_KB
    carry = lax.fori_loop(kb_lo, kb_diag + 1, win_step, init)

    for g in range(B_GROUPS):
        _, l_w, acc_w = carry[g]
        o_w = acc_w / jnp.maximum(l_w, TINY)

        def gate_row(branch):
            base = branch * B_HEADS + g * B_HPG
            return jnp.concatenate([gates[base + hd:base + hd + 1, :] for hd in range(B_HPG)], axis=1)

        o = gate_row(0) * o_cmp[g] + gate_row(1) * o_sel[g] + gate_row(2) * o_w
        for hd in range(B_HPG):
            o_ref[0, head_rows[g][hd], :] = o[:, hd * tq:(hd + 1) * tq].astype(o_ref.dtype)


def _nsa(qt, kc, vct, ka, vt, gates_t):
    bsz, _, seq = qt.shape
    n_cmp = kc.shape[2]
    assert seq // SEL_BLOCK <= BKV_WIDTH - B_DH, "one selection-block id per spare key lane"
    whole = lambda b, i: (b, 0, 0, 0)
    return pl.pallas_call(
        _nsa_body,
        grid=(bsz, seq // NSA_Q),
        in_specs=[
            pl.BlockSpec((1, BQ_WIDTH, NSA_Q), lambda b, i: (b, 0, i)),
            pl.BlockSpec((1, B_GROUPS, n_cmp, B_DH), whole),
            pl.BlockSpec((1, B_GROUPS, B_DH, n_cmp), whole),
            pl.BlockSpec((1, 2 * B_GROUPS, seq, BKV_WIDTH), whole),
            pl.BlockSpec((1, 2 * BKV_WIDTH, seq), lambda b, i: (b, 0, 0)),
            pl.BlockSpec((1, GATE_ROWS, NSA_Q), lambda b, i: (b, 0, i)),
        ],
        out_specs=pl.BlockSpec((1, BQ_WIDTH, NSA_Q), lambda b, i: (b, 0, i)),
        out_shape=jax.ShapeDtypeStruct((bsz, BQ_WIDTH, seq), BF16),
        compiler_params=_params(("parallel", "arbitrary")),
        name="nsa_attn",
    )(qt, kc, vct, ka, vt, gates_t)


def _outproj_body(x_ref, oa_ref, ob_ref, oc_ref, w_ref, o_ref):
    acc = x_ref[...] + _dot(oa_ref[...], w_ref[0:A_WIDTH, :])
    acc = acc + _dot_tn(ob_ref[0], w_ref[A_WIDTH:A_WIDTH + BQ_WIDTH, :])
    acc = acc + _dot_tn(oc_ref[0], w_ref[A_WIDTH + BQ_WIDTH:, :])
    o_ref[...] = acc


def _outproj(x2d, oa2d, ob, oc, w_o, bsz, seq, tb=512):
    n = bsz * seq
    nsb = seq // tb
    row = lambda i: (i, 0)
    return pl.pallas_call(
        _outproj_body,
        grid=(n // tb,),
        in_specs=[
            pl.BlockSpec((tb, D_MODEL), row),
            pl.BlockSpec((tb, A_WIDTH), row),
            pl.BlockSpec((1, BQ_WIDTH, tb), lambda i: (i // nsb, 0, i % nsb)),
            pl.BlockSpec((1, C_WIDTH, tb), lambda i: (i // nsb, 0, i % nsb)),
            pl.BlockSpec((D_MODEL, D_MODEL), lambda i: (0, 0)),
        ],
        out_specs=pl.BlockSpec((tb, D_MODEL), row),
        out_shape=jax.ShapeDtypeStruct((n, D_MODEL), F32),
        compiler_params=_params(("parallel",)),
        name="outproj",
    )(x2d, oa2d, ob, oc, w_o)


FFN_ROWS = 1024
FFN_COLS = 1024


def _ffn_body(x_ref, g_ref, wu_ref, wd_ref, o_ref, h_ref, acc_ref):
    j = pl.program_id(1)

    @pl.when(j == 0)
    def _():
        x = x_ref[...]
        var = jnp.mean(x * x, axis=-1, keepdims=True)
        h_ref[...] = (x * lax.rsqrt(var + EPS) * g_ref[...]).astype(BF16)
        acc_ref[...] = x

    u = jnp.maximum(_dot(h_ref[...], wu_ref[...]), 0.0)
    acc_ref[...] += _dot((u * u).astype(BF16), wd_ref[...])

    @pl.when(j == pl.num_programs(1) - 1)
    def _():
        o_ref[...] = acc_ref[...]


def _ffn(x2d, g, w_up, w_down):
    n = x2d.shape[0]
    return pl.pallas_call(
        _ffn_body,
        grid=(n // FFN_ROWS, D_FF // FFN_COLS),
        in_specs=[
            pl.BlockSpec((FFN_ROWS, D_MODEL), lambda i, j: (i, 0)),
            pl.BlockSpec((1, D_MODEL), lambda i, j: (0, 0)),
            pl.BlockSpec((D_MODEL, FFN_COLS), lambda i, j: (0, j)),
            pl.BlockSpec((FFN_COLS, D_MODEL), lambda i, j: (j, 0)),
        ],
        out_specs=pl.BlockSpec((FFN_ROWS, D_MODEL), lambda i, j: (i, 0)),
        out_shape=jax.ShapeDtypeStruct((n, D_MODEL), F32),
        scratch_shapes=[pltpu.VMEM((FFN_ROWS, D_MODEL), BF16), pltpu.VMEM((FFN_ROWS, D_MODEL), F32)],
        compiler_params=_params(("parallel", "arbitrary")),
        name="ffn",
    )(x2d, g, w_up, w_down)


def _rope_tables(pos, n_heads):
    inv_freq = jnp.power(jnp.float32(ROPE_THETA), -jnp.arange(0, ROT_DIM, 2, dtype=F32) / ROT_DIM)
    ang = pos.astype(F32)[:, None] * inv_freq[None, :]
    cos, sin = jnp.cos(ang), jnp.sin(ang)
    n = pos.shape[0]
    pad = jnp.zeros((n, HEAD_DIM - ROT_DIM), F32)
    zero = jnp.zeros((n, ROT_HALF), F32)
    c = jnp.concatenate([cos, cos, pad + 1.0], axis=1)
    s_lo = jnp.concatenate([zero, sin, pad], axis=1)
    s_hi = jnp.concatenate([-sin, zero, pad], axis=1)
    return tuple(jnp.tile(t, (1, n_heads)) for t in (c, s_lo, s_hi))


def _rope_tables_t(pos):
    inv_freq = jnp.power(jnp.float32(ROPE_THETA), -jnp.arange(0, ROT_DIM, 2, dtype=F32) / ROT_DIM)
    ang = pos.astype(F32)[:, None] * inv_freq[None, :]
    return jnp.cos(ang).T, jnp.sin(ang).T


def _pack_w_in(w):
    n_main = OFF_GATE
    gate_cols = jnp.concatenate(
        [w[:, n_main:n_main + N_GATES], w[:, -C_HEADS:],
         jnp.zeros((D_MODEL, GATE_LANES - N_GATES - C_HEADS), w.dtype)], axis=1)
    return jnp.concatenate([w[:, :n_main], gate_cols, w[:, n_main + N_GATES:-C_HEADS]], axis=1).astype(BF16)


def _expand_groups(w):
    z = jnp.zeros_like(w)
    top = jnp.concatenate([w, z], axis=-1)
    bot = jnp.concatenate([z, w], axis=-1)
    return jnp.concatenate([top, bot], axis=-2)


def kernel(x, norm1_g, w_in, hgrn_lb_logits, hgrn_onorm_g, nsa_qn_g, nsa_kn_g, nsa_cmp_pos, nsa_cmp_w1,
           nsa_cmp_w2, fox_qn_g, fox_kn_g, fox_fb, w_o, norm2_g, w_up, w_down):
    bsz, seq, _ = x.shape
    n = bsz * seq
    n_blk = seq // CMP_STRIDE
    cos, s_lo, s_hi = _rope_tables(jnp.arange(seq), B_GROUPS)
    cos_t, sin_t = _rope_tables_t(jnp.arange(seq))
    ccos, cs_lo, cs_hi = _rope_tables(jnp.arange(n_blk) * CMP_STRIDE + CMP_BLOCK - 1, B_GROUPS)

    x2d = x.reshape(n, D_MODEL)
    for layer in range(DEPTH):
        row = lambda v, reps: jnp.tile(v[layer].astype(F32), reps)[None, :]
        w_all = _pack_w_in(w_in[layer])
        w_t = jnp.concatenate([w_all[:, OFF_BQ:OFF_KC], w_all[:, OFF_VS:OFF_KW], w_all[:, OFF_VW:OFF_GATE],
                               w_all[:, OFF_GATE:OFF_GATE + GATE_ROWS], w_all[:, OFF_CQ:OFF_CK],
                               w_all[:, OFF_CV:PROJ_WIDTH]], axis=1).T
        hg, qt, ka, vt, cmp2d, gates, gates_t, fqt, fk, fvt = _inproj(
            x2d, norm1_g[layer][None, :], w_all, w_t, cos, s_lo, s_hi, cos_t, sin_t,
            nsa_qn_g[layer].astype(F32)[:, None], row(nsa_kn_g, B_GROUPS), fox_qn_g[layer].astype(F32)[:, None],
            row(fox_kn_g, C_HEADS), bsz, seq)

        o_a = _hgrn(hg.reshape(bsz, seq, 4 * A_WIDTH), hgrn_lb_logits.astype(F32),
                    row(hgrn_onorm_g, A_HEADS), layer)

        gates3 = gates.reshape(bsz, seq, GATE_LANES)
        fb_row = jnp.zeros((1, GATE_LANES), F32).at[0, FOX_GATE_LANE0:FOX_GATE_LANE0 + C_HEADS].set(
            fox_fb[layer].astype(F32))
        ck, cq = _foxc(gates3, fb_row)
        o_c = _fox(fqt, cq, fk, ck, fvt)

        w1 = nsa_cmp_w1[layer].reshape(2, CMP_BLOCK, B_DH, CMP_HIDDEN)
        w1_e = _expand_groups(w1).astype(BF16)
        w2_e = _expand_groups(nsa_cmp_w2[layer]).astype(BF16)
        pos_e = jnp.tile(nsa_cmp_pos[layer].astype(F32), (1, 1, B_GROUPS))
        kc, vct = _compress(cmp2d.reshape(2, bsz, seq, BKV_WIDTH), pos_e, w1_e, w2_e[0], w2_e[1].T,
                            row(nsa_kn_g, B_GROUPS), ccos, cs_lo, cs_hi)
        o_b = _nsa(qt, kc, vct, ka, vt, gates_t)

        x2d = _outproj(x2d, o_a.reshape(n, A_WIDTH), o_b, o_c, w_o[layer].astype(BF16), bsz, seq)
        x2d = _ffn(x2d, norm2_g[layer][None, :], w_up[layer].astype(BF16), w_down[layer].astype(BF16))
    return x2d.reshape(bsz, seq, D_MODEL)
```

```python
import functools
import math

import numpy as np
import jax
import jax.numpy as jnp
from jax import lax
from jax.experimental import pallas as pl
from jax.experimental.pallas import tpu as pltpu

F32 = jnp.float32
BF16 = jnp.bfloat16

D_MODEL = 1024
DEPTH = 2
A_HEADS, A_DK, A_DV = 4, 64, 64
B_HEADS, B_GROUPS, B_DH = 8, 2, 64
B_HPG = B_HEADS // B_GROUPS
C_HEADS, C_DH = 4, 64
HEAD_DIM = 64
D_FF = 4 * D_MODEL
ROPE_THETA = 500000.0
ROT_DIM = B_DH // 4
ROT_HALF = ROT_DIM // 2
EPS = 1e-6
NEG_BIG = -1e30
POS_BIG = 1e30
TINY = 1e-30
CMP_BLOCK, CMP_STRIDE, CMP_HIDDEN = 32, 16, 128
SEL_BLOCK, SEL_TOPK, WINDOW = 64, 16, 512

A_WIDTH = A_HEADS * A_DK
BQ_WIDTH = B_HEADS * B_DH
BKV_WIDTH = B_GROUPS * B_DH
C_WIDTH = C_HEADS * C_DH
N_GATES = 3 * B_HEADS
GATE_LANES = 128
FOX_GATE_LANE0 = N_GATES

OFF_A = 0
OFF_BQ = 4 * A_WIDTH
OFF_KC = OFF_BQ + BQ_WIDTH
OFF_VC = OFF_KC + BKV_WIDTH
OFF_KS = OFF_VC + BKV_WIDTH
OFF_VS = OFF_KS + BKV_WIDTH
OFF_KW = OFF_VS + BKV_WIDTH
OFF_VW = OFF_KW + BKV_WIDTH
OFF_GATE = OFF_VW + BKV_WIDTH
OFF_CQ = OFF_GATE + GATE_LANES
OFF_CK = OFF_CQ + C_WIDTH
OFF_CV = OFF_CK + C_WIDTH
PROJ_WIDTH = OFF_CV + C_WIDTH

VMEM_LIMIT = 56 * 1024 * 1024


def _params(sem):
    return pltpu.CompilerParams(dimension_semantics=sem, vmem_limit_bytes=VMEM_LIMIT)


def _dot(a, b):
    return jnp.dot(a, b, preferred_element_type=F32)


def _dot_nt(a, b):
    return lax.dot_general(a, b, (((1,), (1,)), ((), ())), preferred_element_type=F32)


def _dot_tn(a, b):
    return lax.dot_general(a, b, (((0,), (0,)), ((), ())), preferred_element_type=F32)


def _split3(x):
    a = x.astype(BF16)
    r = x - a.astype(F32)
    b = r.astype(BF16)
    c = (r - b.astype(F32)).astype(BF16)
    return a, b, c


def _dot_exact_rhs(m01, x):
    a, b, c = _split3(x)
    return _dot(m01, a) + _dot(m01, b) + _dot(m01, c)


def _iota(shape, dim):
    return lax.broadcasted_iota(jnp.int32, shape, dim)


def _head_ones(width):
    r = _iota((width, width), 0) // HEAD_DIM
    c = _iota((width, width), 1) // HEAD_DIM
    return jnp.where(r == c, 1.0, 0.0).astype(BF16)


def _head_rmsnorm(x, head_ones):
    x2 = x * x
    hi = x2.astype(BF16)
    lo = (x2 - hi.astype(F32)).astype(BF16)
    ms = (_dot(hi, head_ones) + _dot(lo, head_ones)) * (1.0 / HEAD_DIM)
    return x * lax.rsqrt(ms + EPS)


def _rope(x, cos, sin_lo, sin_hi):
    n = x.shape[-1]
    return (x * cos + pltpu.roll(x, ROT_HALF, axis=1) * sin_lo
            + pltpu.roll(x, n - ROT_HALF, axis=1) * sin_hi)


def _sigmoid(x):
    return 1.0 / (1.0 + jnp.exp(-x))


LOG2E = 1.4426950408889634
SUM_ROWS = 16
FIXED_OFFSET_LIMIT = 40.0


def _score_bound(gain_q, gain_k):
    bound = (1.02 * LOG2E * HEAD_DIM ** 0.5) * jnp.max(jnp.abs(gain_q)) * jnp.max(jnp.abs(gain_k))
    fixed = bound < FIXED_OFFSET_LIMIT
    return jnp.stack([jnp.where(fixed, bound, 0.0), jnp.where(fixed, 1.0, 0.0)]).astype(F32)


def _with_sum_rows(vt):
    return jnp.concatenate([vt, jnp.ones((SUM_ROWS, vt.shape[1]), vt.dtype)], axis=0)


def _attn_init(lanes, fixed):
    acc = jnp.zeros((HEAD_DIM + SUM_ROWS, lanes), F32)
    return (acc,) if fixed else (jnp.full((1, lanes), NEG_BIG, F32), acc)


def _attn_step(s, vt_aug, carry, fixed):
    if fixed:
        return (carry[0] + _dot(vt_aug, jnp.exp2(s).astype(BF16)),)
    m_old, acc = carry
    m_new = jnp.maximum(m_old, jnp.max(s, axis=0, keepdims=True))
    p = jnp.exp2(s - m_new)
    return m_new, jnp.exp2(m_old - m_new) * acc + _dot(vt_aug, p.astype(BF16))


def _attn_finish(carry):
    acc = carry[-1]
    return acc[:HEAD_DIM] / jnp.maximum(acc[HEAD_DIM:HEAD_DIM + 1], TINY)


TOFF_Q = 0
TOFF_V = BQ_WIDTH
TOFF_GATE = TOFF_V + 2 * BKV_WIDTH
GATE_ROWS = 32
TOFF_FQ = TOFF_GATE + GATE_ROWS
TOFF_FV = TOFF_FQ + C_WIDTH
T_ROWS = TOFF_FV + C_WIDTH


def _inproj_body(seq_blocks, x_ref, g_ref, w_ref, wt_ref, cos_ref, sl_ref, sh_ref, cost_ref, sint_ref, qg_ref,
                 kg_ref, fqg_ref, fkg_ref, hg_ref, qt_ref, ka_ref, vt_ref, cmp_ref, gt_ref, gtt_ref,
                 fqt_ref, fk_ref, fvt_ref):
    tb = x_ref.shape[0]
    x = x_ref[...]
    var = jnp.mean(x * x, axis=-1, keepdims=True)
    h = (x * lax.rsqrt(var + EPS) * g_ref[...]).astype(BF16)

    def proj(lo, hi):
        return _dot(h, w_ref[:, lo:hi])

    def proj_t(lo, hi):
        return _dot_nt(wt_ref[lo:hi, :], h)

    hg_ref[...] = proj(OFF_A, OFF_BQ)

    q = proj_t(TOFF_Q, TOFF_V).reshape(B_HEADS, B_DH, tb)
    q = q * lax.rsqrt(jnp.mean(q * q, axis=1, keepdims=True) + EPS) * qg_ref[...]
    cost, sint = cost_ref[...], sint_ref[...]
    x1, x2 = q[:, 0:ROT_HALF], q[:, ROT_HALF:ROT_DIM]
    q = jnp.concatenate([x1 * cost - x2 * sint, x2 * cost + x1 * sint, q[:, ROT_DIM:]], axis=1)
    qt_ref[0] = (q * (B_DH ** -0.5 * LOG2E)).reshape(BQ_WIDTH, tb).astype(BF16)

    vt_ref[0] = proj_t(TOFF_V, TOFF_GATE).astype(BF16)
    gtt_ref[0] = proj_t(TOFF_GATE, TOFF_FQ)

    cmp_ref[0] = proj(OFF_KC, OFF_VC)
    cmp_ref[1] = proj(OFF_VC, OFF_KS)

    cos, sl, sh = cos_ref[...], sl_ref[...], sh_ref[...]
    ones_kv = _head_ones(BKV_WIDTH)
    ones_c = _head_ones(C_WIDTH)
    lane = _iota((tb, BKV_WIDTH), 1)
    tok = (pl.program_id(0) % seq_blocks) * tb + _iota((tb, BKV_WIDTH), 0)
    onehot = jnp.where(lane == B_DH + tok // SEL_BLOCK, 1.0, 0.0)
    for slot, off in ((0, OFF_KS), (1, OFF_KW)):
        t = _rope(_head_rmsnorm(proj(off, off + BKV_WIDTH), ones_kv) * kg_ref[...], cos, sl, sh)
        for g in range(B_GROUPS):
            low = t if g == 0 else pltpu.roll(t, B_DH, axis=1)
            ka_ref[0, slot * B_GROUPS + g] = jnp.where(lane < B_DH, low, onehot).astype(BF16)

    gt_ref[...] = proj(OFF_GATE, OFF_CQ)

    fq = proj_t(TOFF_FQ, TOFF_FV).reshape(C_HEADS, C_DH, tb)
    fq = fq * lax.rsqrt(jnp.mean(fq * fq, axis=1, keepdims=True) + EPS) * fqg_ref[...] * (C_DH ** -0.5 * LOG2E)
    fqt_ref[0] = fq.reshape(C_WIDTH, tb).astype(BF16)
    fvt_ref[0] = proj_t(TOFF_FV, T_ROWS).astype(BF16)
    fk = _head_rmsnorm(proj(OFF_CK, OFF_CV), ones_c) * fkg_ref[...]
    for hd in range(C_HEADS):
        slab = fk[:, (hd // 2) * BKV_WIDTH:(hd // 2 + 1) * BKV_WIDTH]
        low = slab if hd % 2 == 0 else pltpu.roll(slab, C_DH, axis=1)
        fk_ref[0, hd] = jnp.where(lane < C_DH, low, 0.0).astype(BF16)


def _inproj(x2d, g, w, wt, cos, sl, sh, cost, sint, qg, kg, fqg, fkg, bsz, seq, tb=256):
    n = bsz * seq
    nsb = seq // tb
    row = lambda i: (i, 0)
    const = lambda i: (0, 0)
    pos = lambda i: (i % nsb, 0)
    pos_t = lambda i: (0, i % nsb)
    heads = lambda i: (i // nsb, 0, i % nsb, 0)
    tokens_on_lanes = lambda i: (i // nsb, 0, i % nsb)
    return pl.pallas_call(
        functools.partial(_inproj_body, nsb),
        grid=(n // tb,),
        in_specs=[
            pl.BlockSpec((tb, D_MODEL), row),
            pl.BlockSpec((1, D_MODEL), const),
            pl.BlockSpec((D_MODEL, PROJ_WIDTH), const),
            pl.BlockSpec((T_ROWS, D_MODEL), const),
            pl.BlockSpec((tb, BKV_WIDTH), pos),
            pl.BlockSpec((tb, BKV_WIDTH), pos),
            pl.BlockSpec((tb, BKV_WIDTH), pos),
            pl.BlockSpec((ROT_HALF, tb), pos_t),
            pl.BlockSpec((ROT_HALF, tb), pos_t),
            pl.BlockSpec((B_DH, 1), const),
            pl.BlockSpec((1, BKV_WIDTH), const),
            pl.BlockSpec((C_DH, 1), const),
            pl.BlockSpec((1, C_WIDTH), const),
        ],
        out_specs=[
            pl.BlockSpec((tb, 4 * A_WIDTH), row),
            pl.BlockSpec((1, BQ_WIDTH, tb), tokens_on_lanes),
            pl.BlockSpec((1, 2 * B_GROUPS, tb, BKV_WIDTH), heads),
            pl.BlockSpec((1, 2 * BKV_WIDTH, tb), tokens_on_lanes),
            pl.BlockSpec((2, tb, BKV_WIDTH), lambda i: (0, i, 0)),
            pl.BlockSpec((tb, GATE_LANES), row),
            pl.BlockSpec((1, GATE_ROWS, tb), tokens_on_lanes),
            pl.BlockSpec((1, C_WIDTH, tb), tokens_on_lanes),
            pl.BlockSpec((1, C_HEADS, tb, BKV_WIDTH), heads),
            pl.BlockSpec((1, C_WIDTH, tb), tokens_on_lanes),
        ],
        out_shape=[
            jax.ShapeDtypeStruct((n, 4 * A_WIDTH), F32),
            jax.ShapeDtypeStruct((bsz, BQ_WIDTH, seq), BF16),
            jax.ShapeDtypeStruct((bsz, 2 * B_GROUPS, seq, BKV_WIDTH), BF16),
            jax.ShapeDtypeStruct((bsz, 2 * BKV_WIDTH, seq), BF16),
            jax.ShapeDtypeStruct((2, n, BKV_WIDTH), F32),
            jax.ShapeDtypeStruct((n, GATE_LANES), F32),
            jax.ShapeDtypeStruct((bsz, GATE_ROWS, seq), F32),
            jax.ShapeDtypeStruct((bsz, C_WIDTH, seq), BF16),
            jax.ShapeDtypeStruct((bsz, C_HEADS, seq, BKV_WIDTH), BF16),
            jax.ShapeDtypeStruct((bsz, C_WIDTH, seq), BF16),
        ],
        compiler_params=_params(("parallel",)),
        name="inproj",
    )(x2d, g, w, wt, cos, sl, sh, cost, sint, qg, kg, fqg, fkg)


HGRN_SUB = 16
HGRN_ROWS = 128


def _hgrn_body(layer, hg_ref, lbl_ref, og_ref, o_ref, st_ref):
    @pl.when(pl.program_id(1) == 0)
    def _():
        st_ref[...] = jnp.zeros_like(st_ref)

    rows, width = HGRN_ROWS, A_WIDTH
    lbl = lbl_ref[...]
    e = jnp.exp(lbl - jnp.max(lbl, axis=0, keepdims=True))
    p = e / jnp.sum(e, axis=0, keepdims=True)
    lb = p[0:1] * 0.0
    for l in range(1, layer + 1):
        lb = lb + p[l:l + 1]

    q = hg_ref[0, :, 0:width] * (A_DK ** -0.5)
    z = hg_ref[0, :, width:2 * width]
    v = hg_ref[0, :, 2 * width:3 * width]
    go = hg_ref[0, :, 3 * width:4 * width]

    f = lb + (1.0 - lb) * _sigmoid(z)
    logf = jnp.log(jnp.maximum(f, TINY))
    k = (1.0 - lb) * _sigmoid(-z)

    r = _iota((rows, rows), 0)
    c = _iota((rows, rows), 1)
    tri = jnp.where((r // HGRN_SUB == c // HGRN_SUB) & (c <= r), 1.0, 0.0).astype(BF16)
    gl = _dot_exact_rhs(tri, logf)

    ones_h = _head_ones(width)
    tin = _iota((rows, width), 0) % HGRN_SUB

    o_band = jnp.zeros((rows, width), F32)
    for d in range(HGRN_SUB):
        if d == 0:
            ks, gs, vs = k, gl, v
        else:
            ks, gs, vs = (pltpu.roll(a, d, axis=0) for a in (k, gl, v))
        valid = tin >= d
        pr = jnp.where(valid, q * ks * jnp.exp(jnp.minimum(gl - gs, 0.0)), 0.0)
        o_band = o_band + _dot(pr.astype(BF16), ones_h) * vs

    hm = _iota((width, width), 0) // HEAD_DIM == _iota((width, width), 1) // HEAD_DIM
    st = st_ref[...]
    outs = []
    for i in range(rows // HGRN_SUB):
        sl_ = slice(i * HGRN_SUB, (i + 1) * HGRN_SUB)
        gi, qi, ki, vi = gl[sl_], q[sl_], k[sl_], v[sl_]
        outs.append(_dot_nt((qi * jnp.exp(gi)).astype(BF16), st.astype(BF16)))
        glast = gi[HGRN_SUB - 1:HGRN_SUB]
        kdec = ki * jnp.exp(glast - gi)
        upd = _dot_tn(vi.astype(BF16), kdec.astype(BF16))
        st = st * jnp.exp(glast) + jnp.where(hm, upd, 0.0)
    st_ref[...] = st
    o = jnp.concatenate(outs, axis=0) + o_band

    o = _head_rmsnorm(o, ones_h) * og_ref[...]
    gate = go * _sigmoid(go)
    o_ref[0] = (o * gate).astype(o_ref.dtype)


def _hgrn(hg3, lb_logits, og, layer):
    bsz, seq, _ = hg3.shape
    return pl.pallas_call(
        functools.partial(_hgrn_body, layer),
        grid=(bsz, seq // HGRN_ROWS),
        in_specs=[
            pl.BlockSpec((1, HGRN_ROWS, 4 * A_WIDTH), lambda b, i: (b, i, 0)),
            pl.BlockSpec((DEPTH, A_WIDTH), lambda b, i: (0, 0)),
            pl.BlockSpec((1, A_WIDTH), lambda b, i: (0, 0)),
        ],
        out_specs=pl.BlockSpec((1, HGRN_ROWS, A_WIDTH), lambda b, i: (b, i, 0)),
        out_shape=jax.ShapeDtypeStruct((bsz, seq, A_WIDTH), BF16),
        scratch_shapes=[pltpu.VMEM((A_WIDTH, A_WIDTH), F32)],
        compiler_params=_params(("parallel", "arbitrary")),
        name="hgrn2",
    )(hg3, lb_logits, og)


FOXC_ROWS = 512


FOX_AUG = 16


def _foxc_body(sc_ref, gt_ref, fb_ref, ck_ref, cq_ref, carry_ref):
    @pl.when(pl.program_id(1) == 0)
    def _():
        carry_ref[...] = jnp.zeros_like(carry_ref)

    rows = FOXC_ROWS
    y = gt_ref[0] + fb_ref[...]
    lf = jnp.minimum(y, 0.0) - jnp.log(1.0 + jnp.exp(-jnp.abs(y)))
    tri = jnp.where(_iota((rows, rows), 1) <= _iota((rows, rows), 0), 1.0, 0.0).astype(BF16)
    cc = _dot_exact_rhs(tri, lf) + carry_ref[...]
    carry_ref[...] = cc[rows - 1:rows]
    parts = jnp.concatenate(_split3(cc * LOG2E), axis=1)

    pr = _iota((3 * GATE_LANES, BKV_WIDTH), 0)
    pc = _iota((3 * GATE_LANES, BKV_WIDTH), 1)
    sr = _iota((FOX_AUG, 3 * GATE_LANES), 0)
    sc = _iota((FOX_AUG, 3 * GATE_LANES), 1)
    k_lane = _iota((rows, BKV_WIDTH), 1)
    ones_k = jnp.where(((k_lane >= C_DH) & (k_lane < C_DH + 3)) | (k_lane == C_DH + 6), 1.0, 0.0)
    q_row = _iota((FOX_AUG, rows), 0)
    ones_q = jnp.where((q_row >= 3) & (q_row < 6), 1.0, 0.0)
    ones_q = jnp.where(q_row == 6, -sc_ref[0], ones_q)
    for hd in range(C_HEADS):
        src = FOX_GATE_LANE0 + hd
        place_k = jnp.where((pr % GATE_LANES == src) & (pc == C_DH + 3 + pr // GATE_LANES), -1.0, 0.0).astype(BF16)
        place_q = jnp.where((sc % GATE_LANES == src) & (sr == sc // GATE_LANES), 1.0, 0.0).astype(BF16)
        ck_ref[0, hd] = (_dot(parts, place_k) + ones_k).astype(BF16)
        cq_ref[0, hd] = (_dot_nt(place_q, parts) + ones_q).astype(BF16)


def _foxc(bound, gates3, fb_row):
    bsz, seq, _ = gates3.shape
    return pl.pallas_call(
        _foxc_body,
        grid=(bsz, seq // FOXC_ROWS),
        in_specs=[
            pl.BlockSpec(memory_space=pltpu.SMEM),
            pl.BlockSpec((1, FOXC_ROWS, GATE_LANES), lambda b, i: (b, i, 0)),
            pl.BlockSpec((1, GATE_LANES), lambda b, i: (0, 0)),
        ],
        out_specs=[
            pl.BlockSpec((1, C_HEADS, FOXC_ROWS, BKV_WIDTH), lambda b, i: (b, 0, i, 0)),
            pl.BlockSpec((1, C_HEADS, FOX_AUG, FOXC_ROWS), lambda b, i: (b, 0, 0, i)),
        ],
        out_shape=[
            jax.ShapeDtypeStruct((bsz, C_HEADS, seq, BKV_WIDTH), BF16),
            jax.ShapeDtypeStruct((bsz, C_HEADS, FOX_AUG, seq), BF16),
        ],
        scratch_shapes=[pltpu.VMEM((1, GATE_LANES), F32)],
        compiler_params=_params(("parallel", "arbitrary")),
        name="fox_cumgate",
    )(bound, gates3, fb_row)


FOX_BLK = 512


def _fox_body(sc_ref, qt_ref, cq_ref, k_ref, ck_ref, vt_ref, o_ref, m_ref, acc_ref):
    qi, kj = pl.program_id(1), pl.program_id(2)
    blk = FOX_BLK
    use_fixed = sc_ref[1] > 0.5

    @pl.when(kj == 0)
    def _():
        m_ref[...] = jnp.full_like(m_ref, NEG_BIG)
        acc_ref[...] = jnp.zeros_like(acc_ref)

    def update(diagonal, fixed):
        pad = jnp.zeros((BKV_WIDTH - C_DH - FOX_AUG, blk), BF16)
        for hd in range(C_HEADS):
            rows = slice(hd * C_DH, (hd + 1) * C_DH)
            q_aug = jnp.concatenate([qt_ref[0, rows, :], cq_ref[0, hd], pad], axis=0)
            s = _dot(k_ref[0, hd] + ck_ref[0, hd], q_aug)
            if diagonal:
                s = jnp.where(_iota((blk, blk), 0) <= _iota((blk, blk), 1), s, NEG_BIG)
            carry = (acc_ref[hd],) if fixed else (m_ref[hd], acc_ref[hd])
            carry = _attn_step(s, _with_sum_rows(vt_ref[0, rows, :]), carry, fixed)
            acc_ref[hd] = carry[-1]
            if not fixed:
                m_ref[hd] = carry[0]
            if diagonal:
                o_ref[0, rows, :] = _attn_finish(carry).astype(o_ref.dtype)

    for fixed in (True, False):
        pick = use_fixed if fixed else jnp.logical_not(use_fixed)
        pl.when(pick & (kj < qi))(functools.partial(update, False, fixed))
        pl.when(pick & (kj == qi))(functools.partial(update, True, fixed))


def _fox(bound, fqt, cq, fk, ck, fvt):
    bsz, _, seq = fqt.shape
    nb = seq // FOX_BLK
    q_lanes = lambda b, i, j: (b, 0, i)
    k_rows = lambda b, i, j: (b, 0, jnp.minimum(i, j), 0)
    return pl.pallas_call(
        _fox_body,
        grid=(bsz, nb, nb),
        in_specs=[
            pl.BlockSpec(memory_space=pltpu.SMEM),
            pl.BlockSpec((1, C_WIDTH, FOX_BLK), q_lanes),
            pl.BlockSpec((1, C_HEADS, FOX_AUG, FOX_BLK), lambda b, i, j: (b, 0, 0, i)),
            pl.BlockSpec((1, C_HEADS, FOX_BLK, BKV_WIDTH), k_rows),
            pl.BlockSpec((1, C_HEADS, FOX_BLK, BKV_WIDTH), k_rows),
            pl.BlockSpec((1, C_WIDTH, FOX_BLK), lambda b, i, j: (b, 0, jnp.minimum(i, j))),
        ],
        out_specs=pl.BlockSpec((1, C_WIDTH, FOX_BLK), q_lanes),
        out_shape=jax.ShapeDtypeStruct((bsz, C_WIDTH, seq), BF16),
        scratch_shapes=[
            pltpu.VMEM((C_HEADS, 1, FOX_BLK), F32),
            pltpu.VMEM((C_HEADS, C_DH + SUM_ROWS, FOX_BLK), F32),
        ],
        compiler_params=_params(("parallel", "parallel", "arbitrary")),
        name="fox_attn",
    )(bound, fqt, cq, fk, ck, fvt)


def _cmp_body(x_ref, pos_ref, w1_ref, w2k_ref, w2vt_ref, kg_ref, cos_ref, sl_ref, sh_ref, kc_ref, vct_ref):
    n_blk = x_ref.shape[2] // CMP_STRIDE
    hw = B_GROUPS * CMP_HIDDEN
    first = [jnp.zeros((n_blk, hw), F32) for _ in range(2)]
    second = [jnp.zeros((n_blk, hw), F32) for _ in range(2)]
    for l in range(CMP_STRIDE):
        for which in range(2):
            xw = x_ref[which, 0, pl.ds(l, n_blk, stride=CMP_STRIDE), :]
            a = (xw + pos_ref[which, l:l + 1, :]).astype(BF16)
            b = (xw + pos_ref[which, CMP_STRIDE + l:CMP_STRIDE + l + 1, :]).astype(BF16)
            first[which] = first[which] + _dot(a, w1_ref[which, l])
            second[which] = second[which] + _dot(b, w1_ref[which, CMP_STRIDE + l])
    hids = []
    for which in range(2):
        hid = first[which] + pltpu.roll(second[which], n_blk - 1, axis=0)
        hid = 0.5 * hid * (1.0 + jnp.tanh(math.sqrt(2.0 / math.pi) * (hid + 0.044715 * hid * hid * hid)))
        hids.append(hid.astype(BF16))
    kc = _dot(hids[0], w2k_ref[...])
    kc = _head_rmsnorm(kc, _head_ones(BKV_WIDTH)) * kg_ref[...]
    kc = _rope(kc, cos_ref[...], sl_ref[...], sh_ref[...])
    vct = _dot_nt(w2vt_ref[...], hids[1])
    for g in range(B_GROUPS):
        kc_ref[0, g] = kc[:, g * B_DH:(g + 1) * B_DH].astype(BF16)
        vct_ref[0, g] = vct[g * B_DH:(g + 1) * B_DH].astype(BF16)


def _compress(cmp4, pos_e, w1_e, w2k_e, w2vt_e, kg, cos, sl, sh):
    _, bsz, seq, _ = cmp4.shape
    n_blk = seq // CMP_STRIDE
    c2 = lambda b: (0, 0)
    c3 = lambda b: (0, 0, 0)
    c4 = lambda b: (0, 0, 0, 0)
    return pl.pallas_call(
        _cmp_body,
        grid=(bsz,),
        in_specs=[
            pl.BlockSpec((2, 1, seq, BKV_WIDTH), lambda b: (0, b, 0, 0)),
            pl.BlockSpec((2, CMP_BLOCK, BKV_WIDTH), c3),
            pl.BlockSpec((2, CMP_BLOCK, BKV_WIDTH, B_GROUPS * CMP_HIDDEN), c4),
            pl.BlockSpec((B_GROUPS * CMP_HIDDEN, BKV_WIDTH), c2),
            pl.BlockSpec((BKV_WIDTH, B_GROUPS * CMP_HIDDEN), c2),
            pl.BlockSpec((1, BKV_WIDTH), c2),
            pl.BlockSpec((n_blk, BKV_WIDTH), c2),
            pl.BlockSpec((n_blk, BKV_WIDTH), c2),
            pl.BlockSpec((n_blk, BKV_WIDTH), c2),
        ],
        out_specs=[
            pl.BlockSpec((1, B_GROUPS, n_blk, B_DH), lambda b: (b, 0, 0, 0)),
            pl.BlockSpec((1, B_GROUPS, B_DH, n_blk), lambda b: (b, 0, 0, 0)),
        ],
        out_shape=[
            jax.ShapeDtypeStruct((bsz, B_GROUPS, n_blk, B_DH), BF16),
            jax.ShapeDtypeStruct((bsz, B_GROUPS, B_DH, n_blk), BF16),
        ],
        compiler_params=_params(("parallel",)),
        name="nsa_compress",
    )(cmp4, pos_e, w1_e, w2k_e, w2vt_e, kg, cos, sl, sh)


NSA_Q = 256
NSA_KB = 256
assert NSA_KB % NSA_Q == 0, "a query block must sit inside one key block (single diagonal step)"


def _nsa_body(sc_ref, qt_ref, kc_ref, vct_ref, ka_ref, vt_ref, gt_ref, o_ref):
    tq = NSA_Q
    lanes = B_HPG * tq
    n_cmp = kc_ref.shape[2]
    n_sel = ka_ref.shape[2] // SEL_BLOCK
    q0 = pl.program_id(1) * tq
    t_lane = q0 + _iota((1, lanes), 1) % tq
    gates = _sigmoid(gt_ref[0])

    cn = _iota((n_sel, n_cmp), 1)
    cj = _iota((n_sel, n_cmp), 0)
    overlap_t = jnp.where((cn * CMP_STRIDE <= cj * SEL_BLOCK + SEL_BLOCK - 1)
                          & (cn * CMP_STRIDE + CMP_BLOCK - 1 >= cj * SEL_BLOCK), 1.0, 0.0).astype(BF16)
    jt = _iota((n_sel, tq), 0)
    row8 = _iota((8, tq), 0)
    tt = q0 + _iota((n_sel, tq), 1)
    cur = tt // SEL_BLOCK
    cmp_end = _iota((n_cmp, lanes), 0) * CMP_STRIDE + CMP_BLOCK - 1
    key_kb = _iota((NSA_KB, lanes), 0)
    kb_diag = q0 // NSA_KB
    offset = sc_ref[0]
    neg_offset = jnp.full((B_DH, lanes), -offset, F32).astype(BF16)

    head_rows = [[slice((g * B_HPG + hd) * B_DH, (g * B_HPG + hd + 1) * B_DH) for hd in range(B_HPG)]
                 for g in range(B_GROUPS)]
    o_cmp, q_sel, q_win = [], [], []
    for g in range(B_GROUPS):
        qt = jnp.concatenate([qt_ref[0, r, :] for r in head_rows[g]], axis=1)

        mask_c = cmp_end <= t_lane
        s_c = jnp.where(mask_c, _dot(kc_ref[0, g], qt), NEG_BIG)
        m_c = jnp.max(s_c, axis=0, keepdims=True)
        p_c = jnp.where(mask_c, jnp.exp2(s_c - m_c), 0.0)
        p_c = p_c / jnp.maximum(jnp.sum(p_c, axis=0, keepdims=True), TINY)
        o_cmp.append(_dot(vct_ref[0, g], p_c.astype(BF16)))

        p_sum = p_c[:, 0:tq]
        for hd in range(1, B_HPG):
            p_sum = p_sum + p_c[:, hd * tq:(hd + 1) * tq]
        p_hi = p_sum.astype(BF16)
        p_lo = (p_sum - p_hi.astype(F32)).astype(BF16)
        imp = _dot(overlap_t, p_hi) + _dot(overlap_t, p_lo)
        imp = jnp.where((jt == 0) | (jt == cur) | (jt == cur - 1), POS_BIG, imp)
        imp = jnp.where(jt * SEL_BLOCK > tt, NEG_BIG, imp)
        tiles = [imp[r:r + 8] for r in range(0, n_sel, 8)]
        ranks = [jnp.zeros((8, tq), F32) for _ in tiles]
        for j2 in range(n_sel):
            other = imp[j2:j2 + 1, :]
            for ti, tile in enumerate(tiles):
                if ti > j2 // 8:
                    beats = other >= tile
                elif ti < j2 // 8:
                    beats = other > tile
                else:
                    beats = (other > tile) | ((other == tile) & (j2 % 8 < row8))
                ranks[ti] = ranks[ti] + jnp.where(beats, 1.0, 0.0)
        rank = jnp.concatenate(ranks, axis=0)
        bias = jnp.where((rank < SEL_TOPK) & (imp > NEG_BIG * 0.5), -offset, NEG_BIG).astype(BF16)
        if n_sel < BKV_WIDTH - B_DH:
            bias = jnp.concatenate([bias, jnp.zeros((BKV_WIDTH - B_DH - n_sel, tq), BF16)], axis=0)
        q_sel.append(jnp.concatenate([qt, jnp.concatenate([bias] * B_HPG, axis=1)], axis=0))
        q_win.append(jnp.concatenate([qt, neg_offset], axis=0))

    def scores(slot, g, start, n_keys, q_aug):
        st = pl.multiple_of(start, NSA_KB)
        s = _dot(ka_ref[0, slot * B_GROUPS + g, pl.ds(st, n_keys), :], q_aug)
        vt = vt_ref[0, (slot * B_GROUPS + g) * B_DH:(slot * B_GROUPS + g + 1) * B_DH, pl.ds(st, n_keys)]
        return s, _with_sum_rows(vt)

    def gate_row(branch, g):
        base = branch * B_HEADS + g * B_HPG
        return jnp.concatenate([gates[base + hd:base + hd + 1, :] for hd in range(B_HPG)], axis=1)

    def attend(fixed):
        init = (_attn_init(lanes, fixed),) * B_GROUPS

        def sel_blocks(kb, carry, n_blocks):
            carry = list(carry)
            for u in range(n_blocks):
                for g in range(B_GROUPS):
                    s, vt = scores(0, g, (kb + u) * NSA_KB, NSA_KB, q_sel[g])
                    carry[g] = _attn_step(s, vt, carry[g], fixed)
            return tuple(carry)

        carry = lax.fori_loop(0, kb_diag // 2, lambda i, c: sel_blocks(2 * i, c, 2), init)
        carry = lax.cond(kb_diag % 2 == 1, lambda c: sel_blocks(kb_diag - 1, c, 1), lambda c: c, carry)
        o_sel = []
        for g in range(B_GROUPS):
            s, vt = scores(0, g, q0, NSA_KB, q_sel[g])
            s = jnp.where(q0 + key_kb <= t_lane, s, NEG_BIG)
            o_sel.append(_attn_finish(_attn_step(s, vt, carry[g], fixed)))

        win_start = jnp.maximum(q0 - WINDOW, 0)
        kpos = win_start + _iota((WINDOW + NSA_Q, lanes), 0)
        mask_w = (kpos <= t_lane) & (kpos > t_lane - WINDOW)
        o_win = []
        for g in range(B_GROUPS):
            s, vt = scores(1, g, win_start, WINDOW + NSA_Q, q_win[g])
            o_win.append(_attn_finish(_attn_step(jnp.where(mask_w, s, NEG_BIG), vt, init[g], fixed)))

        for g in range(B_GROUPS):
            o = gate_row(0, g) * o_cmp[g] + gate_row(1, g) * o_sel[g] + gate_row(2, g) * o_win[g]
            for hd in range(B_HPG):
                o_ref[0, head_rows[g][hd], :] = o[:, hd * tq:(hd + 1) * tq].astype(o_ref.dtype)

    use_fixed = sc_ref[1] > 0.5
    pl.when(use_fixed)(functools.partial(attend, True))
    pl.when(jnp.logical_not(use_fixed))(functools.partial(attend, False))


def _nsa(bound, qt, kc, vct, ka, vt, gates_t):
    bsz, _, seq = qt.shape
    n_cmp = kc.shape[2]
    assert seq // SEL_BLOCK <= BKV_WIDTH - B_DH, "one selection-block id per spare key lane"
    assert seq >= WINDOW + NSA_Q and WINDOW % NSA_KB == 0
    whole = lambda b, i: (b, 0, 0, 0)
    return pl.pallas_call(
        _nsa_body,
        grid=(bsz, seq // NSA_Q),
        in_specs=[
            pl.BlockSpec(memory_space=pltpu.SMEM),
            pl.BlockSpec((1, BQ_WIDTH, NSA_Q), lambda b, i: (b, 0, i)),
            pl.BlockSpec((1, B_GROUPS, n_cmp, B_DH), whole),
            pl.BlockSpec((1, B_GROUPS, B_DH, n_cmp), whole),
            pl.BlockSpec((1, 2 * B_GROUPS, seq, BKV_WIDTH), whole),
            pl.BlockSpec((1, 2 * BKV_WIDTH, seq), lambda b, i: (b, 0, 0)),
            pl.BlockSpec((1, GATE_ROWS, NSA_Q), lambda b, i: (b, 0, i)),
        ],
        out_specs=pl.BlockSpec((1, BQ_WIDTH, NSA_Q), lambda b, i: (b, 0, i)),
        out_shape=jax.ShapeDtypeStruct((bsz, BQ_WIDTH, seq), BF16),
        compiler_params=_params(("parallel", "arbitrary")),
        name="nsa_attn",
    )(bound, qt, kc, vct, ka, vt, gates_t)


def _outproj_body(x_ref, oa_ref, ob_ref, oc_ref, w_ref, o_ref):
    acc = x_ref[...] + _dot(oa_ref[...], w_ref[0:A_WIDTH, :])
    acc = acc + _dot_tn(ob_ref[0], w_ref[A_WIDTH:A_WIDTH + BQ_WIDTH, :])
    acc = acc + _dot_tn(oc_ref[0], w_ref[A_WIDTH + BQ_WIDTH:, :])
    o_ref[...] = acc


def _outproj(x2d, oa2d, ob, oc, w_o, bsz, seq, tb=512):
    n = bsz * seq
    nsb = seq // tb
    row = lambda i: (i, 0)
    return pl.pallas_call(
        _outproj_body,
        grid=(n // tb,),
        in_specs=[
            pl.BlockSpec((tb, D_MODEL), row),
            pl.BlockSpec((tb, A_WIDTH), row),
            pl.BlockSpec((1, BQ_WIDTH, tb), lambda i: (i // nsb, 0, i % nsb)),
            pl.BlockSpec((1, C_WIDTH, tb), lambda i: (i // nsb, 0, i % nsb)),
            pl.BlockSpec((D_MODEL, D_MODEL), lambda i: (0, 0)),
        ],
        out_specs=pl.BlockSpec((tb, D_MODEL), row),
        out_shape=jax.ShapeDtypeStruct((n, D_MODEL), F32),
        compiler_params=_params(("parallel",)),
        name="outproj",
    )(x2d, oa2d, ob, oc, w_o)


FFN_ROWS = 1024
FFN_COLS = 1024


def _ffn_body(x_ref, g_ref, wu_ref, wd_ref, o_ref, h_ref, acc_ref):
    j = pl.program_id(1)

    @pl.when(j == 0)
    def _():
        x = x_ref[...]
        var = jnp.mean(x * x, axis=-1, keepdims=True)
        h_ref[...] = (x * lax.rsqrt(var + EPS) * g_ref[...]).astype(BF16)
        acc_ref[...] = x

    u = jnp.maximum(_dot(h_ref[...], wu_ref[...]), 0.0)
    acc_ref[...] += _dot((u * u).astype(BF16), wd_ref[...])

    @pl.when(j == pl.num_programs(1) - 1)
    def _():
        o_ref[...] = acc_ref[...]


def _ffn(x2d, g, w_up, w_down):
    n = x2d.shape[0]
    return pl.pallas_call(
        _ffn_body,
        grid=(n // FFN_ROWS, D_FF // FFN_COLS),
        in_specs=[
            pl.BlockSpec((FFN_ROWS, D_MODEL), lambda i, j: (i, 0)),
            pl.BlockSpec((1, D_MODEL), lambda i, j: (0, 0)),
            pl.BlockSpec((D_MODEL, FFN_COLS), lambda i, j: (0, j)),
            pl.BlockSpec((FFN_COLS, D_MODEL), lambda i, j: (j, 0)),
        ],
        out_specs=pl.BlockSpec((FFN_ROWS, D_MODEL), lambda i, j: (i, 0)),
        out_shape=jax.ShapeDtypeStruct((n, D_MODEL), F32),
        scratch_shapes=[pltpu.VMEM((FFN_ROWS, D_MODEL), BF16), pltpu.VMEM((FFN_ROWS, D_MODEL), F32)],
        compiler_params=_params(("parallel", "arbitrary")),
        name="ffn",
    )(x2d, g, w_up, w_down)


def _rope_tables(pos, n_heads):
    inv_freq = jnp.power(jnp.float32(ROPE_THETA), -jnp.arange(0, ROT_DIM, 2, dtype=F32) / ROT_DIM)
    ang = pos.astype(F32)[:, None] * inv_freq[None, :]
    cos, sin = jnp.cos(ang), jnp.sin(ang)
    n = pos.shape[0]
    pad = jnp.zeros((n, HEAD_DIM - ROT_DIM), F32)
    zero = jnp.zeros((n, ROT_HALF), F32)
    c = jnp.concatenate([cos, cos, pad + 1.0], axis=1)
    s_lo = jnp.concatenate([zero, sin, pad], axis=1)
    s_hi = jnp.concatenate([-sin, zero, pad], axis=1)
    return tuple(jnp.tile(t, (1, n_heads)) for t in (c, s_lo, s_hi))


def _rope_tables_t(pos):
    inv_freq = jnp.power(jnp.float32(ROPE_THETA), -jnp.arange(0, ROT_DIM, 2, dtype=F32) / ROT_DIM)
    ang = pos.astype(F32)[:, None] * inv_freq[None, :]
    return jnp.cos(ang).T, jnp.sin(ang).T


def _pack_w_in(w):
    n_main = OFF_GATE
    gate_cols = jnp.concatenate(
        [w[:, n_main:n_main + N_GATES], w[:, -C_HEADS:],
         jnp.zeros((D_MODEL, GATE_LANES - N_GATES - C_HEADS), w.dtype)], axis=1)
    return jnp.concatenate([w[:, :n_main], gate_cols, w[:, n_main + N_GATES:-C_HEADS]], axis=1).astype(BF16)


def _expand_groups(w):
    z = jnp.zeros_like(w)
    top = jnp.concatenate([w, z], axis=-1)
    bot = jnp.concatenate([z, w], axis=-1)
    return jnp.concatenate([top, bot], axis=-2)


def kernel(x, norm1_g, w_in, hgrn_lb_logits, hgrn_onorm_g, nsa_qn_g, nsa_kn_g, nsa_cmp_pos, nsa_cmp_w1,
           nsa_cmp_w2, fox_qn_g, fox_kn_g, fox_fb, w_o, norm2_g, w_up, w_down):
    bsz, seq, _ = x.shape
    n = bsz * seq
    n_blk = seq // CMP_STRIDE
    cos, s_lo, s_hi = _rope_tables(jnp.arange(seq), B_GROUPS)
    cos_t, sin_t = _rope_tables_t(jnp.arange(seq))
    ccos, cs_lo, cs_hi = _rope_tables(jnp.arange(n_blk) * CMP_STRIDE + CMP_BLOCK - 1, B_GROUPS)

    x2d = x.reshape(n, D_MODEL)
    for layer in range(DEPTH):
        row = lambda v, reps: jnp.tile(v[layer].astype(F32), reps)[None, :]
        w_all = _pack_w_in(w_in[layer])
        w_t = jnp.concatenate([w_all[:, OFF_BQ:OFF_KC], w_all[:, OFF_VS:OFF_KW], w_all[:, OFF_VW:OFF_GATE],
                               w_all[:, OFF_GATE:OFF_GATE + GATE_ROWS], w_all[:, OFF_CQ:OFF_CK],
                               w_all[:, OFF_CV:PROJ_WIDTH]], axis=1).T
        hg, qt, ka, vt, cmp2d, gates, gates_t, fqt, fk, fvt = _inproj(
            x2d, norm1_g[layer][None, :], w_all, w_t, cos, s_lo, s_hi, cos_t, sin_t,
            nsa_qn_g[layer].astype(F32)[:, None], row(nsa_kn_g, B_GROUPS), fox_qn_g[layer].astype(F32)[:, None],
            row(fox_kn_g, C_HEADS), bsz, seq)

        o_a = _hgrn(hg.reshape(bsz, seq, 4 * A_WIDTH), hgrn_lb_logits.astype(F32),
                    row(hgrn_onorm_g, A_HEADS), layer)

        gates3 = gates.reshape(bsz, seq, GATE_LANES)
        fb_row = jnp.zeros((1, GATE_LANES), F32).at[0, FOX_GATE_LANE0:FOX_GATE_LANE0 + C_HEADS].set(
            fox_fb[layer].astype(F32))
        fox_bound = _score_bound(fox_qn_g[layer], fox_kn_g[layer])
        ck, cq = _foxc(fox_bound, gates3, fb_row)
        o_c = _fox(fox_bound, fqt, cq, fk, ck, fvt)

        w1 = nsa_cmp_w1[layer].reshape(2, CMP_BLOCK, B_DH, CMP_HIDDEN)
        w1_e = _expand_groups(w1).astype(BF16)
        w2_e = _expand_groups(nsa_cmp_w2[layer]).astype(BF16)
        pos_e = jnp.tile(nsa_cmp_pos[layer].astype(F32), (1, 1, B_GROUPS))
        kc, vct = _compress(cmp2d.reshape(2, bsz, seq, BKV_WIDTH), pos_e, w1_e, w2_e[0], w2_e[1].T,
                            row(nsa_kn_g, B_GROUPS), ccos, cs_lo, cs_hi)
        o_b = _nsa(_score_bound(nsa_qn_g[layer], nsa_kn_g[layer]), qt, kc, vct, ka, vt, gates_t)

        x2d = _outproj(x2d, o_a.reshape(n, A_WIDTH), o_b, o_c, w_o[layer].astype(BF16), bsz, seq)
        x2d = _ffn(x2d, norm2_g[layer][None, :], w_up[layer].astype(BF16), w_down[layer].astype(BF16))
    return x2d.reshape(bsz, seq, D_MODEL)
```

```python
import functools
import math

import numpy as np
import jax
import jax.numpy as jnp
from jax import lax
from jax.experimental import pallas as pl
from jax.experimental.pallas import tpu as pltpu

F32 = jnp.float32
BF16 = jnp.bfloat16

D_MODEL = 1024
DEPTH = 2
A_HEADS, A_DK, A_DV = 4, 64, 64
B_HEADS, B_GROUPS, B_DH = 8, 2, 64
B_HPG = B_HEADS // B_GROUPS
C_HEADS, C_DH = 4, 64
HEAD_DIM = 64
D_FF = 4 * D_MODEL
ROPE_THETA = 500000.0
ROT_DIM = B_DH // 4
ROT_HALF = ROT_DIM // 2
EPS = 1e-6
NEG_BIG = -1e30
POS_BIG = 1e30
TINY = 1e-30
CMP_BLOCK, CMP_STRIDE, CMP_HIDDEN = 32, 16, 128
SEL_BLOCK, SEL_TOPK, WINDOW = 64, 16, 512

A_WIDTH = A_HEADS * A_DK
BQ_WIDTH = B_HEADS * B_DH
BKV_WIDTH = B_GROUPS * B_DH
C_WIDTH = C_HEADS * C_DH
N_GATES = 3 * B_HEADS
GATE_LANES = 128
FOX_GATE_LANE0 = N_GATES

SRC_A = 0
SRC_BQ = 4 * A_WIDTH
SRC_KC = SRC_BQ + BQ_WIDTH
SRC_VC = SRC_KC + BKV_WIDTH
SRC_KS = SRC_VC + BKV_WIDTH
SRC_VS = SRC_KS + BKV_WIDTH
SRC_KW = SRC_VS + BKV_WIDTH
SRC_VW = SRC_KW + BKV_WIDTH
SRC_BG = SRC_VW + BKV_WIDTH
SRC_CQ = SRC_BG + N_GATES
SRC_CK = SRC_CQ + C_WIDTH
SRC_CV = SRC_CK + C_WIDTH
SRC_CF = SRC_CV + C_WIDTH

OFF_A = 0
OFF_KC = 4 * A_WIDTH
OFF_VC = OFF_KC + BKV_WIDTH
OFF_KS = OFF_VC + BKV_WIDTH
OFF_KW = OFF_KS + BKV_WIDTH
OFF_GATE = OFF_KW + BKV_WIDTH
OFF_CK = OFF_GATE + GATE_LANES
PROJ_WIDTH = OFF_CK + C_WIDTH

VMEM_LIMIT = 56 * 1024 * 1024


def _params(sem):
    return pltpu.CompilerParams(dimension_semantics=sem, vmem_limit_bytes=VMEM_LIMIT)


def _dot(a, b):
    return jnp.dot(a, b, preferred_element_type=F32)


def _dot_nt(a, b):
    return lax.dot_general(a, b, (((1,), (1,)), ((), ())), preferred_element_type=F32)


def _dot_tn(a, b):
    return lax.dot_general(a, b, (((0,), (0,)), ((), ())), preferred_element_type=F32)


def _split3(x):
    a = x.astype(BF16)
    r = x - a.astype(F32)
    b = r.astype(BF16)
    c = (r - b.astype(F32)).astype(BF16)
    return a, b, c


def _dot_exact_rhs(m01, x):
    a, b, c = _split3(x)
    return _dot(m01, a) + _dot(m01, b) + _dot(m01, c)


def _iota(shape, dim):
    return lax.broadcasted_iota(jnp.int32, shape, dim)


def _head_ones(width):
    r = _iota((width, width), 0) // HEAD_DIM
    c = _iota((width, width), 1) // HEAD_DIM
    return jnp.where(r == c, 1.0, 0.0).astype(BF16)


def _head_rmsnorm(x, head_ones):
    x2 = x * x
    hi = x2.astype(BF16)
    lo = (x2 - hi.astype(F32)).astype(BF16)
    ms = (_dot(hi, head_ones) + _dot(lo, head_ones)) * (1.0 / HEAD_DIM)
    return x * lax.rsqrt(ms + EPS)


def _rope(x, cos, sin_lo, sin_hi):
    n = x.shape[-1]
    return (x * cos + pltpu.roll(x, ROT_HALF, axis=1) * sin_lo
            + pltpu.roll(x, n - ROT_HALF, axis=1) * sin_hi)


def _sigmoid(x):
    return 1.0 / (1.0 + jnp.exp(-x))


LOG2E = 1.4426950408889634
SUM_ROWS = 16
FIXED_OFFSET_LIMIT = 40.0


def _score_bound(gain_q, gain_k):
    bound = (1.02 * LOG2E * HEAD_DIM ** 0.5) * jnp.max(jnp.abs(gain_q)) * jnp.max(jnp.abs(gain_k))
    fixed = bound < FIXED_OFFSET_LIMIT
    return jnp.stack([jnp.where(fixed, bound, 0.0), jnp.where(fixed, 1.0, 0.0)]).astype(F32)


def _with_sum_rows(vt):
    return jnp.concatenate([vt, jnp.ones((SUM_ROWS, vt.shape[1]), vt.dtype)], axis=0)


def _attn_init(lanes, fixed):
    acc = jnp.zeros((HEAD_DIM + SUM_ROWS, lanes), F32)
    return (acc,) if fixed else (jnp.full((1, lanes), NEG_BIG, F32), acc)


def _attn_step(s, vt_aug, carry, fixed):
    if fixed:
        return (carry[0] + _dot(vt_aug, jnp.exp2(s).astype(BF16)),)
    m_old, acc = carry
    m_new = jnp.maximum(m_old, jnp.max(s, axis=0, keepdims=True))
    p = jnp.exp2(s - m_new)
    return m_new, jnp.exp2(m_old - m_new) * acc + _dot(vt_aug, p.astype(BF16))


def _attn_finish(carry):
    acc = carry[-1]
    return acc[:HEAD_DIM] / jnp.maximum(acc[HEAD_DIM:HEAD_DIM + 1], TINY)


TOFF_Q = 0
TOFF_V = BQ_WIDTH
TOFF_GATE = TOFF_V + 2 * BKV_WIDTH
GATE_ROWS = 32
TOFF_FQ = TOFF_GATE + GATE_ROWS
TOFF_FV = TOFF_FQ + C_WIDTH
T_ROWS = TOFF_FV + C_WIDTH


def _inproj_body(seq_blocks, x_ref, g_ref, w_ref, wt_ref, cos_ref, sl_ref, sh_ref, cost_ref, sint_ref, qg_ref,
                 kg_ref, fqg_ref, fkg_ref, hg_ref, qt_ref, ka_ref, vt_ref, cmp_ref, gt_ref, gtt_ref,
                 fqt_ref, fk_ref, fvt_ref):
    tb = x_ref.shape[0]
    x = x_ref[...]
    var = jnp.mean(x * x, axis=-1, keepdims=True)
    h = (x * lax.rsqrt(var + EPS) * g_ref[...]).astype(BF16)

    def proj(lo, hi):
        return _dot(h, w_ref[:, lo:hi])

    def proj_t(lo, hi):
        return _dot_nt(wt_ref[lo:hi, :], h)

    hg_ref[...] = proj(OFF_A, OFF_KC)

    q = proj_t(TOFF_Q, TOFF_V).reshape(B_HEADS, B_DH, tb)
    q = q * lax.rsqrt(jnp.mean(q * q, axis=1, keepdims=True) + EPS) * qg_ref[...]
    cost, sint = cost_ref[...], sint_ref[...]
    x1, x2 = q[:, 0:ROT_HALF], q[:, ROT_HALF:ROT_DIM]
    q = jnp.concatenate([x1 * cost - x2 * sint, x2 * cost + x1 * sint, q[:, ROT_DIM:]], axis=1)
    qt_ref[0] = (q * (B_DH ** -0.5 * LOG2E)).reshape(BQ_WIDTH, tb).astype(BF16)

    vt_ref[0] = proj_t(TOFF_V, TOFF_GATE).astype(BF16)
    gtt_ref[0] = proj_t(TOFF_GATE, TOFF_FQ)

    cmp_ref[0] = proj(OFF_KC, OFF_VC)
    cmp_ref[1] = proj(OFF_VC, OFF_KS)

    cos, sl, sh = cos_ref[...], sl_ref[...], sh_ref[...]
    ones_kv = _head_ones(BKV_WIDTH)
    ones_c = _head_ones(C_WIDTH)
    lane = _iota((tb, BKV_WIDTH), 1)
    tok = (pl.program_id(0) % seq_blocks) * tb + _iota((tb, BKV_WIDTH), 0)
    onehot = jnp.where(lane == B_DH + tok // SEL_BLOCK, 1.0, 0.0)
    for slot, off in ((0, OFF_KS), (1, OFF_KW)):
        t = _rope(_head_rmsnorm(proj(off, off + BKV_WIDTH), ones_kv) * kg_ref[...], cos, sl, sh)
        for g in range(B_GROUPS):
            low = t if g == 0 else pltpu.roll(t, B_DH, axis=1)
            ka_ref[0, slot * B_GROUPS + g] = jnp.where(lane < B_DH, low, onehot).astype(BF16)

    gt_ref[...] = proj(OFF_GATE, OFF_CK)

    fq = proj_t(TOFF_FQ, TOFF_FV).reshape(C_HEADS, C_DH, tb)
    fq = fq * lax.rsqrt(jnp.mean(fq * fq, axis=1, keepdims=True) + EPS) * fqg_ref[...] * (C_DH ** -0.5 * LOG2E)
    fqt_ref[0] = fq.reshape(C_WIDTH, tb).astype(BF16)
    fvt_ref[0] = proj_t(TOFF_FV, T_ROWS).astype(BF16)
    fk = _head_rmsnorm(proj(OFF_CK, PROJ_WIDTH), ones_c) * fkg_ref[...]
    for hd in range(C_HEADS):
        slab = fk[:, (hd // 2) * BKV_WIDTH:(hd // 2 + 1) * BKV_WIDTH]
        low = slab if hd % 2 == 0 else pltpu.roll(slab, C_DH, axis=1)
        fk_ref[0, hd] = jnp.where(lane < C_DH, low, 0.0).astype(BF16)


def _inproj(x2d, g, w, wt, cos, sl, sh, cost, sint, qg, kg, fqg, fkg, bsz, seq, tb=512):
    n = bsz * seq
    nsb = seq // tb
    row = lambda i: (i, 0)
    const = lambda i: (0, 0)
    pos = lambda i: (i % nsb, 0)
    pos_t = lambda i: (0, i % nsb)
    heads = lambda i: (i // nsb, 0, i % nsb, 0)
    tokens_on_lanes = lambda i: (i // nsb, 0, i % nsb)
    return pl.pallas_call(
        functools.partial(_inproj_body, nsb),
        grid=(n // tb,),
        in_specs=[
            pl.BlockSpec((tb, D_MODEL), row),
            pl.BlockSpec((1, D_MODEL), const),
            pl.BlockSpec((D_MODEL, PROJ_WIDTH), const),
            pl.BlockSpec((T_ROWS, D_MODEL), const),
            pl.BlockSpec((tb, BKV_WIDTH), pos),
            pl.BlockSpec((tb, BKV_WIDTH), pos),
            pl.BlockSpec((tb, BKV_WIDTH), pos),
            pl.BlockSpec((ROT_HALF, tb), pos_t),
            pl.BlockSpec((ROT_HALF, tb), pos_t),
            pl.BlockSpec((B_DH, 1), const),
            pl.BlockSpec((1, BKV_WIDTH), const),
            pl.BlockSpec((C_DH, 1), const),
            pl.BlockSpec((1, C_WIDTH), const),
        ],
        out_specs=[
            pl.BlockSpec((tb, 4 * A_WIDTH), row),
            pl.BlockSpec((1, BQ_WIDTH, tb), tokens_on_lanes),
            pl.BlockSpec((1, 2 * B_GROUPS, tb, BKV_WIDTH), heads),
            pl.BlockSpec((1, 2 * BKV_WIDTH, tb), tokens_on_lanes),
            pl.BlockSpec((2, tb, BKV_WIDTH), lambda i: (0, i, 0)),
            pl.BlockSpec((tb, GATE_LANES), row),
            pl.BlockSpec((1, GATE_ROWS, tb), tokens_on_lanes),
            pl.BlockSpec((1, C_WIDTH, tb), tokens_on_lanes),
            pl.BlockSpec((1, C_HEADS, tb, BKV_WIDTH), heads),
            pl.BlockSpec((1, C_WIDTH, tb), tokens_on_lanes),
        ],
        out_shape=[
            jax.ShapeDtypeStruct((n, 4 * A_WIDTH), F32),
            jax.ShapeDtypeStruct((bsz, BQ_WIDTH, seq), BF16),
            jax.ShapeDtypeStruct((bsz, 2 * B_GROUPS, seq, BKV_WIDTH), BF16),
            jax.ShapeDtypeStruct((bsz, 2 * BKV_WIDTH, seq), BF16),
            jax.ShapeDtypeStruct((2, n, BKV_WIDTH), F32),
            jax.ShapeDtypeStruct((n, GATE_LANES), F32),
            jax.ShapeDtypeStruct((bsz, GATE_ROWS, seq), F32),
            jax.ShapeDtypeStruct((bsz, C_WIDTH, seq), BF16),
            jax.ShapeDtypeStruct((bsz, C_HEADS, seq, BKV_WIDTH), BF16),
            jax.ShapeDtypeStruct((bsz, C_WIDTH, seq), BF16),
        ],
        compiler_params=_params(("parallel",)),
        name="inproj",
    )(x2d, g, w, wt, cos, sl, sh, cost, sint, qg, kg, fqg, fkg)


HGRN_SUB = 16
HGRN_ROWS = 128


def _hgrn_body(layer, hg_ref, lbl_ref, og_ref, o_ref, st_ref):
    @pl.when(pl.program_id(1) == 0)
    def _():
        st_ref[...] = jnp.zeros_like(st_ref)

    rows, width = HGRN_ROWS, A_WIDTH
    lbl = lbl_ref[...]
    e = jnp.exp(lbl - jnp.max(lbl, axis=0, keepdims=True))
    p = e / jnp.sum(e, axis=0, keepdims=True)
    lb = p[0:1] * 0.0
    for l in range(1, layer + 1):
        lb = lb + p[l:l + 1]

    q = hg_ref[0, :, 0:width] * (A_DK ** -0.5)
    z = hg_ref[0, :, width:2 * width]
    v = hg_ref[0, :, 2 * width:3 * width]
    go = hg_ref[0, :, 3 * width:4 * width]

    f = lb + (1.0 - lb) * _sigmoid(z)
    logf = jnp.log(jnp.maximum(f, TINY))
    k = (1.0 - lb) * _sigmoid(-z)

    r = _iota((rows, rows), 0)
    c = _iota((rows, rows), 1)
    tri = jnp.where((r // HGRN_SUB == c // HGRN_SUB) & (c <= r), 1.0, 0.0).astype(BF16)
    gl = _dot_exact_rhs(tri, logf * LOG2E)

    ones_h = _head_ones(width)

    tin = _iota((rows, width), 0) % HGRN_SUB
    o_band = jnp.zeros((rows, width), F32)
    for d in range(HGRN_SUB):
        ks, gs, vs = (k, gl, v) if d == 0 else (pltpu.roll(a, d, axis=0) for a in (k, gl, v))
        pr = q * ks * jnp.exp2(jnp.where(tin >= d, gl - gs, NEG_BIG))
        o_band = o_band + _dot(pr.astype(BF16), ones_h) * vs

    head_of_lane = _iota((HGRN_SUB, width), 1) // HEAD_DIM

    def per_head(a):
        return jnp.concatenate([jnp.where(head_of_lane == hd, a, 0.0) for hd in range(A_HEADS)], axis=0).astype(BF16)

    st = st_ref[...]
    outs = []
    for i in range(rows // HGRN_SUB):
        sl_ = slice(i * HGRN_SUB, (i + 1) * HGRN_SUB)
        gi, qi, ki, vi = gl[sl_], q[sl_], k[sl_], v[sl_]
        outs.append(_dot_nt((qi * jnp.exp2(gi)).astype(BF16), st.astype(BF16)))
        glast = gi[HGRN_SUB - 1:HGRN_SUB]
        kdec = ki * jnp.exp2(glast - gi)
        st = st * jnp.exp2(glast) + _dot_tn(per_head(vi), per_head(kdec))
    st_ref[...] = st
    o = jnp.concatenate(outs, axis=0) + o_band

    o = _head_rmsnorm(o, ones_h) * og_ref[...]
    gate = go * _sigmoid(go)
    o_ref[0] = (o * gate).astype(o_ref.dtype)


def _hgrn(hg3, lb_logits, og, layer):
    bsz, seq, _ = hg3.shape
    return pl.pallas_call(
        functools.partial(_hgrn_body, layer),
        grid=(bsz, seq // HGRN_ROWS),
        in_specs=[
            pl.BlockSpec((1, HGRN_ROWS, 4 * A_WIDTH), lambda b, i: (b, i, 0)),
            pl.BlockSpec((DEPTH, A_WIDTH), lambda b, i: (0, 0)),
            pl.BlockSpec((1, A_WIDTH), lambda b, i: (0, 0)),
        ],
        out_specs=pl.BlockSpec((1, HGRN_ROWS, A_WIDTH), lambda b, i: (b, i, 0)),
        out_shape=jax.ShapeDtypeStruct((bsz, seq, A_WIDTH), BF16),
        scratch_shapes=[pltpu.VMEM((A_WIDTH, A_WIDTH), F32)],
        compiler_params=_params(("parallel", "arbitrary")),
        name="hgrn2",
    )(hg3, lb_logits, og)


FOXC_ROWS = 512


FOX_AUG = 16


def _foxc_body(sc_ref, gt_ref, fb_ref, ck_ref, cq_ref, carry_ref):
    @pl.when(pl.program_id(1) == 0)
    def _():
        carry_ref[...] = jnp.zeros_like(carry_ref)

    rows = FOXC_ROWS
    y = gt_ref[0] + fb_ref[...]
    lf = jnp.minimum(y, 0.0) - jnp.log(1.0 + jnp.exp(-jnp.abs(y)))
    tri = jnp.where(_iota((rows, rows), 1) <= _iota((rows, rows), 0), 1.0, 0.0).astype(BF16)
    cc = _dot_exact_rhs(tri, lf) + carry_ref[...]
    carry_ref[...] = cc[rows - 1:rows]
    parts = jnp.concatenate(_split3(cc * LOG2E), axis=1)

    pr = _iota((3 * GATE_LANES, BKV_WIDTH), 0)
    pc = _iota((3 * GATE_LANES, BKV_WIDTH), 1)
    sr = _iota((FOX_AUG, 3 * GATE_LANES), 0)
    sc = _iota((FOX_AUG, 3 * GATE_LANES), 1)
    k_lane = _iota((rows, BKV_WIDTH), 1)
    ones_k = jnp.where(((k_lane >= C_DH) & (k_lane < C_DH + 3)) | (k_lane == C_DH + 6), 1.0, 0.0)
    q_row = _iota((FOX_AUG, rows), 0)
    ones_q = jnp.where((q_row >= 3) & (q_row < 6), 1.0, 0.0)
    ones_q = jnp.where(q_row == 6, -sc_ref[0], ones_q)
    for hd in range(C_HEADS):
        src = FOX_GATE_LANE0 + hd
        place_k = jnp.where((pr % GATE_LANES == src) & (pc == C_DH + 3 + pr // GATE_LANES), -1.0, 0.0).astype(BF16)
        place_q = jnp.where((sc % GATE_LANES == src) & (sr == sc // GATE_LANES), 1.0, 0.0).astype(BF16)
        ck_ref[0, hd] = (_dot(parts, place_k) + ones_k).astype(BF16)
        cq_ref[0, hd] = (_dot_nt(place_q, parts) + ones_q).astype(BF16)


def _foxc(bound, gates3, fb_row):
    bsz, seq, _ = gates3.shape
    return pl.pallas_call(
        _foxc_body,
        grid=(bsz, seq // FOXC_ROWS),
        in_specs=[
            pl.BlockSpec(memory_space=pltpu.SMEM),
            pl.BlockSpec((1, FOXC_ROWS, GATE_LANES), lambda b, i: (b, i, 0)),
            pl.BlockSpec((1, GATE_LANES), lambda b, i: (0, 0)),
        ],
        out_specs=[
            pl.BlockSpec((1, C_HEADS, FOXC_ROWS, BKV_WIDTH), lambda b, i: (b, 0, i, 0)),
            pl.BlockSpec((1, C_HEADS, FOX_AUG, FOXC_ROWS), lambda b, i: (b, 0, 0, i)),
        ],
        out_shape=[
            jax.ShapeDtypeStruct((bsz, C_HEADS, seq, BKV_WIDTH), BF16),
            jax.ShapeDtypeStruct((bsz, C_HEADS, FOX_AUG, seq), BF16),
        ],
        scratch_shapes=[pltpu.VMEM((1, GATE_LANES), F32)],
        compiler_params=_params(("parallel", "arbitrary")),
        name="fox_cumgate",
    )(bound, gates3, fb_row)


FOX_BLK = 512


def _fox_body(qi_ref, kj_ref, sc_ref, qt_ref, cq_ref, k_ref, ck_ref, vt_ref, o_ref, m_ref, acc_ref):
    qi, kj = qi_ref[pl.program_id(1)], kj_ref[pl.program_id(1)]
    blk = FOX_BLK
    use_fixed = sc_ref[1] > 0.5

    @pl.when(kj == 0)
    def _():
        m_ref[...] = jnp.full_like(m_ref, NEG_BIG)
        acc_ref[...] = jnp.zeros_like(acc_ref)

    def update(diagonal, fixed):
        pad = jnp.zeros((BKV_WIDTH - C_DH - FOX_AUG, blk), BF16)
        for hd in range(C_HEADS):
            rows = slice(hd * C_DH, (hd + 1) * C_DH)
            q_aug = jnp.concatenate([qt_ref[0, rows, :], cq_ref[0, hd], pad], axis=0)
            s = _dot(k_ref[0, hd] + ck_ref[0, hd], q_aug)
            if diagonal:
                s = jnp.where(_iota((blk, blk), 0) <= _iota((blk, blk), 1), s, NEG_BIG)
            carry = (acc_ref[hd],) if fixed else (m_ref[hd], acc_ref[hd])
            carry = _attn_step(s, _with_sum_rows(vt_ref[0, rows, :]), carry, fixed)
            acc_ref[hd] = carry[-1]
            if not fixed:
                m_ref[hd] = carry[0]
            if diagonal:
                o_ref[0, rows, :] = _attn_finish(carry).astype(o_ref.dtype)

    for fixed in (True, False):
        pick = use_fixed if fixed else jnp.logical_not(use_fixed)
        pl.when(pick & (kj < qi))(functools.partial(update, False, fixed))
        pl.when(pick & (kj == qi))(functools.partial(update, True, fixed))


def _fox(bound, fqt, cq, fk, ck, fvt):
    bsz, _, seq = fqt.shape
    nb = seq // FOX_BLK
    pairs = [(i, j) for i in range(nb) for j in range(i + 1)]
    qi = jnp.asarray([p[0] for p in pairs], jnp.int32)
    kj = jnp.asarray([p[1] for p in pairs], jnp.int32)
    q_lanes = lambda b, p, qi, kj: (b, 0, qi[p])
    k_rows = lambda b, p, qi, kj: (b, 0, kj[p], 0)
    grid_spec = pltpu.PrefetchScalarGridSpec(
        num_scalar_prefetch=2,
        grid=(bsz, len(pairs)),
        in_specs=[
            pl.BlockSpec(memory_space=pltpu.SMEM),
            pl.BlockSpec((1, C_WIDTH, FOX_BLK), q_lanes),
            pl.BlockSpec((1, C_HEADS, FOX_AUG, FOX_BLK), lambda b, p, qi, kj: (b, 0, 0, qi[p])),
            pl.BlockSpec((1, C_HEADS, FOX_BLK, BKV_WIDTH), k_rows),
            pl.BlockSpec((1, C_HEADS, FOX_BLK, BKV_WIDTH), k_rows),
            pl.BlockSpec((1, C_WIDTH, FOX_BLK), lambda b, p, qi, kj: (b, 0, kj[p])),
        ],
        out_specs=pl.BlockSpec((1, C_WIDTH, FOX_BLK), q_lanes),
        scratch_shapes=[
            pltpu.VMEM((C_HEADS, 1, FOX_BLK), F32),
            pltpu.VMEM((C_HEADS, C_DH + SUM_ROWS, FOX_BLK), F32),
        ],
    )
    return pl.pallas_call(
        _fox_body,
        grid_spec=grid_spec,
        out_shape=jax.ShapeDtypeStruct((bsz, C_WIDTH, seq), BF16),
        compiler_params=_params(("parallel", "arbitrary")),
        name="fox_attn",
    )(qi, kj, bound, fqt, cq, fk, ck, fvt)


def _cmp_body(x_ref, pos_ref, w1_ref, w2k_ref, w2vt_ref, kg_ref, cos_ref, sl_ref, sh_ref, kc_ref, vct_ref):
    n_blk = x_ref.shape[2] // CMP_STRIDE
    hw = B_GROUPS * CMP_HIDDEN
    first = [jnp.zeros((n_blk, hw), F32) for _ in range(2)]
    second = [jnp.zeros((n_blk, hw), F32) for _ in range(2)]
    for l in range(CMP_STRIDE):
        for which in range(2):
            xw = x_ref[which, 0, pl.ds(l, n_blk, stride=CMP_STRIDE), :]
            a = (xw + pos_ref[which, l:l + 1, :]).astype(BF16)
            b = (xw + pos_ref[which, CMP_STRIDE + l:CMP_STRIDE + l + 1, :]).astype(BF16)
            first[which] = first[which] + _dot(a, w1_ref[which, l])
            second[which] = second[which] + _dot(b, w1_ref[which, CMP_STRIDE + l])
    hids = []
    for which in range(2):
        hid = first[which] + pltpu.roll(second[which], n_blk - 1, axis=0)
        hid = 0.5 * hid * (1.0 + jnp.tanh(math.sqrt(2.0 / math.pi) * (hid + 0.044715 * hid * hid * hid)))
        hids.append(hid.astype(BF16))
    kc = _dot(hids[0], w2k_ref[...])
    kc = _head_rmsnorm(kc, _head_ones(BKV_WIDTH)) * kg_ref[...]
    kc = _rope(kc, cos_ref[...], sl_ref[...], sh_ref[...])
    vct = _dot_nt(w2vt_ref[...], hids[1])
    for g in range(B_GROUPS):
        kc_ref[0, g] = kc[:, g * B_DH:(g + 1) * B_DH].astype(BF16)
        vct_ref[0, g] = vct[g * B_DH:(g + 1) * B_DH].astype(BF16)


def _compress(cmp4, pos_e, w1_e, w2k_e, w2vt_e, kg, cos, sl, sh):
    _, bsz, seq, _ = cmp4.shape
    n_blk = seq // CMP_STRIDE
    c2 = lambda b: (0, 0)
    c3 = lambda b: (0, 0, 0)
    c4 = lambda b: (0, 0, 0, 0)
    return pl.pallas_call(
        _cmp_body,
        grid=(bsz,),
        in_specs=[
            pl.BlockSpec((2, 1, seq, BKV_WIDTH), lambda b: (0, b, 0, 0)),
            pl.BlockSpec((2, CMP_BLOCK, BKV_WIDTH), c3),
            pl.BlockSpec((2, CMP_BLOCK, BKV_WIDTH, B_GROUPS * CMP_HIDDEN), c4),
            pl.BlockSpec((B_GROUPS * CMP_HIDDEN, BKV_WIDTH), c2),
            pl.BlockSpec((BKV_WIDTH, B_GROUPS * CMP_HIDDEN), c2),
            pl.BlockSpec((1, BKV_WIDTH), c2),
            pl.BlockSpec((n_blk, BKV_WIDTH), c2),
            pl.BlockSpec((n_blk, BKV_WIDTH), c2),
            pl.BlockSpec((n_blk, BKV_WIDTH), c2),
        ],
        out_specs=[
            pl.BlockSpec((1, B_GROUPS, n_blk, B_DH), lambda b: (b, 0, 0, 0)),
            pl.BlockSpec((1, B_GROUPS, B_DH, n_blk), lambda b: (b, 0, 0, 0)),
        ],
        out_shape=[
            jax.ShapeDtypeStruct((bsz, B_GROUPS, n_blk, B_DH), BF16),
            jax.ShapeDtypeStruct((bsz, B_GROUPS, B_DH, n_blk), BF16),
        ],
        compiler_params=_params(("parallel",)),
        name="nsa_compress",
    )(cmp4, pos_e, w1_e, w2k_e, w2vt_e, kg, cos, sl, sh)


NSA_Q = 256
NSA_KB = 256
assert NSA_KB % NSA_Q == 0, "a query block must sit inside one key block (single diagonal step)"


def _nsa_body(sc_ref, qt_ref, kc_ref, vct_ref, ka_ref, vt_ref, gt_ref, o_ref):
    tq = NSA_Q
    lanes = B_HPG * tq
    n_cmp = kc_ref.shape[2]
    n_sel = ka_ref.shape[2] // SEL_BLOCK
    q0 = pl.program_id(1) * tq
    t_lane = q0 + _iota((1, lanes), 1) % tq
    gates = _sigmoid(gt_ref[0])

    cn = _iota((n_sel, n_cmp), 1)
    cj = _iota((n_sel, n_cmp), 0)
    overlap_t = jnp.where((cn * CMP_STRIDE <= cj * SEL_BLOCK + SEL_BLOCK - 1)
                          & (cn * CMP_STRIDE + CMP_BLOCK - 1 >= cj * SEL_BLOCK), 1.0, 0.0).astype(BF16)
    jt = _iota((n_sel, tq), 0)
    row8 = _iota((8, tq), 0)
    tt = q0 + _iota((n_sel, tq), 1)
    cur = tt // SEL_BLOCK
    cmp_end = _iota((n_cmp, lanes), 0) * CMP_STRIDE + CMP_BLOCK - 1
    key_kb = _iota((NSA_KB, lanes), 0)
    kb_diag = q0 // NSA_KB
    offset = sc_ref[0]
    neg_offset = jnp.full((B_DH, lanes), -offset, F32).astype(BF16)

    head_rows = [[slice((g * B_HPG + hd) * B_DH, (g * B_HPG + hd + 1) * B_DH) for hd in range(B_HPG)]
                 for g in range(B_GROUPS)]
    o_cmp, q_sel, q_win = [], [], []
    for g in range(B_GROUPS):
        qt = jnp.concatenate([qt_ref[0, r, :] for r in head_rows[g]], axis=1)

        mask_c = cmp_end <= t_lane
        s_c = jnp.where(mask_c, _dot(kc_ref[0, g], qt), NEG_BIG)
        m_c = jnp.max(s_c, axis=0, keepdims=True)
        p_c = jnp.where(mask_c, jnp.exp2(s_c - m_c), 0.0)
        p_c = p_c / jnp.maximum(jnp.sum(p_c, axis=0, keepdims=True), TINY)
        o_cmp.append(_dot(vct_ref[0, g], p_c.astype(BF16)))

        p_sum = p_c[:, 0:tq]
        for hd in range(1, B_HPG):
            p_sum = p_sum + p_c[:, hd * tq:(hd + 1) * tq]
        p_hi = p_sum.astype(BF16)
        p_lo = (p_sum - p_hi.astype(F32)).astype(BF16)
        imp = _dot(overlap_t, p_hi) + _dot(overlap_t, p_lo)
        imp = jnp.where((jt == 0) | (jt == cur) | (jt == cur - 1), POS_BIG, imp)
        imp = jnp.where(jt * SEL_BLOCK > tt, NEG_BIG, imp)
        tiles = [imp[r:r + 8] for r in range(0, n_sel, 8)]
        ranks = [jnp.zeros((8, tq), F32) for _ in tiles]
        for j2 in range(n_sel):
            other = imp[j2:j2 + 1, :]
            for ti, tile in enumerate(tiles):
                if ti > j2 // 8:
                    beats = other >= tile
                elif ti < j2 // 8:
                    beats = other > tile
                else:
                    beats = (other > tile) | ((other == tile) & (j2 % 8 < row8))
                ranks[ti] = ranks[ti] + jnp.where(beats, 1.0, 0.0)
        rank = jnp.concatenate(ranks, axis=0)
        bias = jnp.where((rank < SEL_TOPK) & (imp > NEG_BIG * 0.5), -offset, NEG_BIG).astype(BF16)
        if n_sel < BKV_WIDTH - B_DH:
            bias = jnp.concatenate([bias, jnp.zeros((BKV_WIDTH - B_DH - n_sel, tq), BF16)], axis=0)
        q_sel.append(jnp.concatenate([qt, jnp.concatenate([bias] * B_HPG, axis=1)], axis=0))
        q_win.append(jnp.concatenate([qt, neg_offset], axis=0))

    def scores(slot, g, start, n_keys, q_aug):
        st = pl.multiple_of(start, NSA_KB)
        s = _dot(ka_ref[0, slot * B_GROUPS + g, pl.ds(st, n_keys), :], q_aug)
        vt = vt_ref[0, (slot * B_GROUPS + g) * B_DH:(slot * B_GROUPS + g + 1) * B_DH, pl.ds(st, n_keys)]
        return s, _with_sum_rows(vt)

    def gate_row(branch, g):
        base = branch * B_HEADS + g * B_HPG
        return jnp.concatenate([gates[base + hd:base + hd + 1, :] for hd in range(B_HPG)], axis=1)

    def attend(fixed):
        init = (_attn_init(lanes, fixed),) * B_GROUPS

        def sel_blocks(kb, carry, n_blocks):
            carry = list(carry)
            for u in range(n_blocks):
                for g in range(B_GROUPS):
                    s, vt = scores(0, g, (kb + u) * NSA_KB, NSA_KB, q_sel[g])
                    carry[g] = _attn_step(s, vt, carry[g], fixed)
            return tuple(carry)

        carry = lax.fori_loop(0, kb_diag // 2, lambda i, c: sel_blocks(2 * i, c, 2), init)
        carry = lax.cond(kb_diag % 2 == 1, lambda c: sel_blocks(kb_diag - 1, c, 1), lambda c: c, carry)
        o_sel = []
        for g in range(B_GROUPS):
            s, vt = scores(0, g, q0, NSA_KB, q_sel[g])
            s = jnp.where(q0 + key_kb <= t_lane, s, NEG_BIG)
            o_sel.append(_attn_finish(_attn_step(s, vt, carry[g], fixed)))

        win_start = jnp.maximum(q0 - WINDOW, 0)
        kpos = win_start + _iota((WINDOW + NSA_Q, lanes), 0)
        mask_w = (kpos <= t_lane) & (kpos > t_lane - WINDOW)
        o_win = []
        for g in range(B_GROUPS):
            s, vt = scores(1, g, win_start, WINDOW + NSA_Q, q_win[g])
            o_win.append(_attn_finish(_attn_step(jnp.where(mask_w, s, NEG_BIG), vt, init[g], fixed)))

        for g in range(B_GROUPS):
            o = gate_row(0, g) * o_cmp[g] + gate_row(1, g) * o_sel[g] + gate_row(2, g) * o_win[g]
            for hd in range(B_HPG):
                o_ref[0, head_rows[g][hd], :] = o[:, hd * tq:(hd + 1) * tq].astype(o_ref.dtype)

    use_fixed = sc_ref[1] > 0.5
    pl.when(use_fixed)(functools.partial(attend, True))
    pl.when(jnp.logical_not(use_fixed))(functools.partial(attend, False))


def _nsa(bound, qt, kc, vct, ka, vt, gates_t):
    bsz, _, seq = qt.shape
    n_cmp = kc.shape[2]
    assert seq // SEL_BLOCK <= BKV_WIDTH - B_DH, "one selection-block id per spare key lane"
    assert seq >= WINDOW + NSA_Q and WINDOW % NSA_KB == 0
    whole = lambda b, i: (b, 0, 0, 0)
    return pl.pallas_call(
        _nsa_body,
        grid=(bsz, seq // NSA_Q),
        in_specs=[
            pl.BlockSpec(memory_space=pltpu.SMEM),
            pl.BlockSpec((1, BQ_WIDTH, NSA_Q), lambda b, i: (b, 0, i)),
            pl.BlockSpec((1, B_GROUPS, n_cmp, B_DH), whole),
            pl.BlockSpec((1, B_GROUPS, B_DH, n_cmp), whole),
            pl.BlockSpec((1, 2 * B_GROUPS, seq, BKV_WIDTH), whole),
            pl.BlockSpec((1, 2 * BKV_WIDTH, seq), lambda b, i: (b, 0, 0)),
            pl.BlockSpec((1, GATE_ROWS, NSA_Q), lambda b, i: (b, 0, i)),
        ],
        out_specs=pl.BlockSpec((1, BQ_WIDTH, NSA_Q), lambda b, i: (b, 0, i)),
        out_shape=jax.ShapeDtypeStruct((bsz, BQ_WIDTH, seq), BF16),
        compiler_params=_params(("parallel", "arbitrary")),
        name="nsa_attn",
    )(bound, qt, kc, vct, ka, vt, gates_t)


FFN_ROWS = 1024
FFN_COLS = 1024


def _mix_ffn_body(x_ref, oa_ref, ob_ref, oc_ref, wo_ref, g_ref, wu_ref, wd_ref, o_ref, h_ref, acc_ref):
    j = pl.program_id(1)

    @pl.when(j == 0)
    def _():
        x = x_ref[...] + _dot(oa_ref[...], wo_ref[0:A_WIDTH, :])
        x = x + _dot_tn(ob_ref[0], wo_ref[A_WIDTH:A_WIDTH + BQ_WIDTH, :])
        x = x + _dot_tn(oc_ref[0], wo_ref[A_WIDTH + BQ_WIDTH:, :])
        var = jnp.mean(x * x, axis=-1, keepdims=True)
        h_ref[...] = (x * lax.rsqrt(var + EPS) * g_ref[...]).astype(BF16)
        acc_ref[...] = x

    u = jnp.maximum(_dot(h_ref[...], wu_ref[...]), 0.0)
    acc_ref[...] += _dot((u * u).astype(BF16), wd_ref[...])

    @pl.when(j == pl.num_programs(1) - 1)
    def _():
        o_ref[...] = acc_ref[...]


def _mix_ffn(x2d, oa2d, ob, oc, w_o, g, w_up, w_down, bsz, seq):
    n = bsz * seq
    nsb = seq // FFN_ROWS
    row = lambda i, j: (i, 0)
    const = lambda i, j: (0, 0)
    tokens_on_lanes = lambda i, j: (i // nsb, 0, i % nsb)
    return pl.pallas_call(
        _mix_ffn_body,
        grid=(n // FFN_ROWS, D_FF // FFN_COLS),
        in_specs=[
            pl.BlockSpec((FFN_ROWS, D_MODEL), row),
            pl.BlockSpec((FFN_ROWS, A_WIDTH), row),
            pl.BlockSpec((1, BQ_WIDTH, FFN_ROWS), tokens_on_lanes),
            pl.BlockSpec((1, C_WIDTH, FFN_ROWS), tokens_on_lanes),
            pl.BlockSpec((D_MODEL, D_MODEL), const),
            pl.BlockSpec((1, D_MODEL), const),
            pl.BlockSpec((D_MODEL, FFN_COLS), lambda i, j: (0, j)),
            pl.BlockSpec((FFN_COLS, D_MODEL), lambda i, j: (j, 0)),
        ],
        out_specs=pl.BlockSpec((FFN_ROWS, D_MODEL), row),
        out_shape=jax.ShapeDtypeStruct((n, D_MODEL), F32),
        scratch_shapes=[pltpu.VMEM((FFN_ROWS, D_MODEL), BF16), pltpu.VMEM((FFN_ROWS, D_MODEL), F32)],
        compiler_params=_params(("parallel", "arbitrary")),
        name="mix_ffn",
    )(x2d, oa2d, ob, oc, w_o, g, w_up, w_down)


def _rope_tables(pos, n_heads):
    inv_freq = jnp.power(jnp.float32(ROPE_THETA), -jnp.arange(0, ROT_DIM, 2, dtype=F32) / ROT_DIM)
    ang = pos.astype(F32)[:, None] * inv_freq[None, :]
    cos, sin = jnp.cos(ang), jnp.sin(ang)
    n = pos.shape[0]
    pad = jnp.zeros((n, HEAD_DIM - ROT_DIM), F32)
    zero = jnp.zeros((n, ROT_HALF), F32)
    c = jnp.concatenate([cos, cos, pad + 1.0], axis=1)
    s_lo = jnp.concatenate([zero, sin, pad], axis=1)
    s_hi = jnp.concatenate([-sin, zero, pad], axis=1)
    return tuple(jnp.tile(t, (1, n_heads)) for t in (c, s_lo, s_hi))


def _rope_tables_t(pos):
    inv_freq = jnp.power(jnp.float32(ROPE_THETA), -jnp.arange(0, ROT_DIM, 2, dtype=F32) / ROT_DIM)
    ang = pos.astype(F32)[:, None] * inv_freq[None, :]
    return jnp.cos(ang).T, jnp.sin(ang).T


def _pack_w_in(w):
    w = w.astype(BF16)
    cols = lambda lo, width: w[:, lo:lo + width]
    gates = jnp.concatenate([cols(SRC_BG, N_GATES), cols(SRC_CF, C_HEADS)], axis=1)
    pad = lambda width: jnp.zeros((D_MODEL, width - N_GATES - C_HEADS), BF16)
    w_rows = jnp.concatenate([cols(SRC_A, 4 * A_WIDTH), cols(SRC_KC, 2 * BKV_WIDTH), cols(SRC_KS, BKV_WIDTH),
                              cols(SRC_KW, BKV_WIDTH), gates, pad(GATE_LANES), cols(SRC_CK, C_WIDTH)], axis=1)
    w_lanes = jnp.concatenate([cols(SRC_BQ, BQ_WIDTH), cols(SRC_VS, BKV_WIDTH), cols(SRC_VW, BKV_WIDTH),
                               gates, pad(GATE_ROWS), cols(SRC_CQ, C_WIDTH), cols(SRC_CV, C_WIDTH)], axis=1)
    return w_rows, w_lanes.T


def _expand_groups(w):
    z = jnp.zeros_like(w)
    top = jnp.concatenate([w, z], axis=-1)
    bot = jnp.concatenate([z, w], axis=-1)
    return jnp.concatenate([top, bot], axis=-2)


def kernel(x, norm1_g, w_in, hgrn_lb_logits, hgrn_onorm_g, nsa_qn_g, nsa_kn_g, nsa_cmp_pos, nsa_cmp_w1,
           nsa_cmp_w2, fox_qn_g, fox_kn_g, fox_fb, w_o, norm2_g, w_up, w_down):
    bsz, seq, _ = x.shape
    n = bsz * seq
    n_blk = seq // CMP_STRIDE
    cos, s_lo, s_hi = _rope_tables(jnp.arange(seq), B_GROUPS)
    cos_t, sin_t = _rope_tables_t(jnp.arange(seq))
    ccos, cs_lo, cs_hi = _rope_tables(jnp.arange(n_blk) * CMP_STRIDE + CMP_BLOCK - 1, B_GROUPS)

    x2d = x.reshape(n, D_MODEL)
    for layer in range(DEPTH):
        row = lambda v, reps: jnp.tile(v[layer].astype(F32), reps)[None, :]
        w_rows, w_lanes = _pack_w_in(w_in[layer])
        hg, qt, ka, vt, cmp2d, gates, gates_t, fqt, fk, fvt = _inproj(
            x2d, norm1_g[layer][None, :], w_rows, w_lanes, cos, s_lo, s_hi, cos_t, sin_t,
            nsa_qn_g[layer].astype(F32)[:, None], row(nsa_kn_g, B_GROUPS), fox_qn_g[layer].astype(F32)[:, None],
            row(fox_kn_g, C_HEADS), bsz, seq)

        o_a = _hgrn(hg.reshape(bsz, seq, 4 * A_WIDTH), hgrn_lb_logits.astype(F32),
                    row(hgrn_onorm_g, A_HEADS), layer)

        gates3 = gates.reshape(bsz, seq, GATE_LANES)
        fb_row = jnp.zeros((1, GATE_LANES), F32).at[0, FOX_GATE_LANE0:FOX_GATE_LANE0 + C_HEADS].set(
            fox_fb[layer].astype(F32))
        fox_bound = _score_bound(fox_qn_g[layer], fox_kn_g[layer])
        ck, cq = _foxc(fox_bound, gates3, fb_row)
        o_c = _fox(fox_bound, fqt, cq, fk, ck, fvt)

        w1 = nsa_cmp_w1[layer].reshape(2, CMP_BLOCK, B_DH, CMP_HIDDEN)
        w1_e = _expand_groups(w1).astype(BF16)
        w2_e = _expand_groups(nsa_cmp_w2[layer]).astype(BF16)
        pos_e = jnp.tile(nsa_cmp_pos[layer].astype(F32), (1, 1, B_GROUPS))
        kc, vct = _compress(cmp2d.reshape(2, bsz, seq, BKV_WIDTH), pos_e, w1_e, w2_e[0], w2_e[1].T,
                            row(nsa_kn_g, B_GROUPS), ccos, cs_lo, cs_hi)
        o_b = _nsa(_score_bound(nsa_qn_g[layer], nsa_kn_g[layer]), qt, kc, vct, ka, vt, gates_t)

        x2d = _mix_ffn(x2d, o_a.reshape(n, A_WIDTH), o_b, o_c, w_o[layer].astype(BF16), norm2_g[layer][None, :],
                       w_up[layer].astype(BF16), w_down[layer].astype(BF16), bsz, seq)
    return x2d.reshape(bsz, seq, D_MODEL)
```

```python
import functools
import math

import numpy as np
import jax
import jax.numpy as jnp
from jax import lax
from jax.experimental import pallas as pl
from jax.experimental.pallas import tpu as pltpu

F32 = jnp.float32
BF16 = jnp.bfloat16

D_MODEL = 1024
DEPTH = 2
A_HEADS, A_DK, A_DV = 4, 64, 64
B_HEADS, B_GROUPS, B_DH = 8, 2, 64
B_HPG = B_HEADS // B_GROUPS
C_HEADS, C_DH = 4, 64
HEAD_DIM = 64
D_FF = 4 * D_MODEL
ROPE_THETA = 500000.0
ROT_DIM = B_DH // 4
ROT_HALF = ROT_DIM // 2
EPS = 1e-6
NEG_BIG = -1e30
POS_BIG = 1e30
TINY = 1e-30
CMP_BLOCK, CMP_STRIDE, CMP_HIDDEN = 32, 16, 128
SEL_BLOCK, SEL_TOPK, WINDOW = 64, 16, 512

A_WIDTH = A_HEADS * A_DK
BQ_WIDTH = B_HEADS * B_DH
BKV_WIDTH = B_GROUPS * B_DH
C_WIDTH = C_HEADS * C_DH
N_GATES = 3 * B_HEADS
GATE_LANES = 128
FOX_GATE_LANE0 = N_GATES

SRC_A = 0
SRC_BQ = 4 * A_WIDTH
SRC_KC = SRC_BQ + BQ_WIDTH
SRC_VC = SRC_KC + BKV_WIDTH
SRC_KS = SRC_VC + BKV_WIDTH
SRC_VS = SRC_KS + BKV_WIDTH
SRC_KW = SRC_VS + BKV_WIDTH
SRC_VW = SRC_KW + BKV_WIDTH
SRC_BG = SRC_VW + BKV_WIDTH
SRC_CQ = SRC_BG + N_GATES
SRC_CK = SRC_CQ + C_WIDTH
SRC_CV = SRC_CK + C_WIDTH
SRC_CF = SRC_CV + C_WIDTH

OFF_A = 0
OFF_KC = 4 * A_WIDTH
OFF_VC = OFF_KC + BKV_WIDTH
OFF_KS = OFF_VC + BKV_WIDTH
OFF_KW = OFF_KS + BKV_WIDTH
OFF_GATE = OFF_KW + BKV_WIDTH
OFF_CK = OFF_GATE + GATE_LANES
PROJ_WIDTH = OFF_CK + C_WIDTH

VMEM_LIMIT = 56 * 1024 * 1024


def _params(sem):
    return pltpu.CompilerParams(dimension_semantics=sem, vmem_limit_bytes=VMEM_LIMIT)


def _dot(a, b):
    return jnp.dot(a, b, preferred_element_type=F32)


def _dot_nt(a, b):
    return lax.dot_general(a, b, (((1,), (1,)), ((), ())), preferred_element_type=F32)


def _dot_tn(a, b):
    return lax.dot_general(a, b, (((0,), (0,)), ((), ())), preferred_element_type=F32)


def _split3(x):
    a = x.astype(BF16)
    r = x - a.astype(F32)
    b = r.astype(BF16)
    c = (r - b.astype(F32)).astype(BF16)
    return a, b, c


def _dot_exact_rhs(m01, x):
    a, b, c = _split3(x)
    return _dot(m01, a) + _dot(m01, b) + _dot(m01, c)


def _iota(shape, dim):
    return lax.broadcasted_iota(jnp.int32, shape, dim)


def _head_ones(width):
    r = _iota((width, width), 0) // HEAD_DIM
    c = _iota((width, width), 1) // HEAD_DIM
    return jnp.where(r == c, 1.0, 0.0).astype(BF16)


def _head_rmsnorm(x, head_ones):
    x2 = x * x
    hi = x2.astype(BF16)
    lo = (x2 - hi.astype(F32)).astype(BF16)
    ms = (_dot(hi, head_ones) + _dot(lo, head_ones)) * (1.0 / HEAD_DIM)
    return x * lax.rsqrt(ms + EPS)


def _rope(x, cos, sin_lo, sin_hi):
    n = x.shape[-1]
    return (x * cos + pltpu.roll(x, ROT_HALF, axis=1) * sin_lo
            + pltpu.roll(x, n - ROT_HALF, axis=1) * sin_hi)


def _sigmoid(x):
    return 1.0 / (1.0 + jnp.exp(-x))


LOG2E = 1.4426950408889634
SUM_ROWS = 16
FIXED_OFFSET_LIMIT = 40.0


def _score_bound(gain_q, gain_k):
    bound = (1.02 * LOG2E * HEAD_DIM ** 0.5) * jnp.max(jnp.abs(gain_q)) * jnp.max(jnp.abs(gain_k))
    fixed = bound < FIXED_OFFSET_LIMIT
    return jnp.stack([jnp.where(fixed, bound, 0.0), jnp.where(fixed, 1.0, 0.0)]).astype(F32)


def _with_sum_rows(vt):
    return jnp.concatenate([vt, jnp.ones((SUM_ROWS, vt.shape[1]), vt.dtype)], axis=0)


def _attn_init(lanes, fixed):
    acc = jnp.zeros((HEAD_DIM + SUM_ROWS, lanes), F32)
    return (acc,) if fixed else (jnp.full((1, lanes), NEG_BIG, F32), acc)


def _attn_step(s, vt_aug, carry, fixed):
    if fixed:
        return (carry[0] + _dot(vt_aug, jnp.exp2(s).astype(BF16)),)
    m_old, acc = carry
    m_new = jnp.maximum(m_old, jnp.max(s, axis=0, keepdims=True))
    p = jnp.exp2(s - m_new)
    return m_new, jnp.exp2(m_old - m_new) * acc + _dot(vt_aug, p.astype(BF16))


def _attn_finish(carry):
    acc = carry[-1]
    return acc[:HEAD_DIM] / jnp.maximum(acc[HEAD_DIM:HEAD_DIM + 1], TINY)


TOFF_Q = 0
TOFF_V = BQ_WIDTH
TOFF_GATE = TOFF_V + 2 * BKV_WIDTH
GATE_ROWS = 32
TOFF_FQ = TOFF_GATE + GATE_ROWS
TOFF_FV = TOFF_FQ + C_WIDTH
T_ROWS = TOFF_FV + C_WIDTH


def _inproj_body(seq_blocks, x_ref, g_ref, w_ref, wt_ref, cos_ref, sl_ref, sh_ref, cost_ref, sint_ref, qg_ref,
                 kg_ref, fqg_ref, fkg_ref, hg_ref, qt_ref, ka_ref, vt_ref, cmp_ref, gt_ref, gtt_ref,
                 fqt_ref, fk_ref, fvt_ref):
    tb = x_ref.shape[0]
    x = x_ref[...]
    var = jnp.mean(x * x, axis=-1, keepdims=True)
    h = (x * lax.rsqrt(var + EPS) * g_ref[...]).astype(BF16)

    def proj(lo, hi):
        return _dot(h, w_ref[:, lo:hi])

    def proj_t(lo, hi):
        return _dot_nt(wt_ref[lo:hi, :], h)

    hg_ref[...] = proj(OFF_A, OFF_KC)

    q = proj_t(TOFF_Q, TOFF_V).reshape(B_HEADS, B_DH, tb)
    q = q * lax.rsqrt(jnp.mean(q * q, axis=1, keepdims=True) + EPS) * qg_ref[...]
    cost, sint = cost_ref[...], sint_ref[...]
    x1, x2 = q[:, 0:ROT_HALF], q[:, ROT_HALF:ROT_DIM]
    q = jnp.concatenate([x1 * cost - x2 * sint, x2 * cost + x1 * sint, q[:, ROT_DIM:]], axis=1)
    qt_ref[0] = (q * (B_DH ** -0.5 * LOG2E)).reshape(BQ_WIDTH, tb).astype(BF16)

    vt_ref[0] = proj_t(TOFF_V, TOFF_GATE).astype(BF16)
    gtt_ref[0] = proj_t(TOFF_GATE, TOFF_FQ)

    cmp_ref[0] = proj(OFF_KC, OFF_VC)
    cmp_ref[1] = proj(OFF_VC, OFF_KS)

    cos, sl, sh = cos_ref[...], sl_ref[...], sh_ref[...]
    ones_kv = _head_ones(BKV_WIDTH)
    ones_c = _head_ones(C_WIDTH)
    lane = _iota((tb, BKV_WIDTH), 1)
    tok = (pl.program_id(0) % seq_blocks) * tb + _iota((tb, BKV_WIDTH), 0)
    onehot = jnp.where(lane == B_DH + tok // SEL_BLOCK, 1.0, 0.0)
    for slot, off in ((0, OFF_KS), (1, OFF_KW)):
        t = _rope(_head_rmsnorm(proj(off, off + BKV_WIDTH), ones_kv) * kg_ref[...], cos, sl, sh)
        for g in range(B_GROUPS):
            low = t if g == 0 else pltpu.roll(t, B_DH, axis=1)
            ka_ref[0, slot * B_GROUPS + g] = jnp.where(lane < B_DH, low, onehot).astype(BF16)

    gt_ref[...] = proj(OFF_GATE, OFF_CK)

    fq = proj_t(TOFF_FQ, TOFF_FV).reshape(C_HEADS, C_DH, tb)
    fq = fq * lax.rsqrt(jnp.mean(fq * fq, axis=1, keepdims=True) + EPS) * fqg_ref[...] * (C_DH ** -0.5 * LOG2E)
    fqt_ref[0] = fq.reshape(C_WIDTH, tb).astype(BF16)
    fvt_ref[0] = proj_t(TOFF_FV, T_ROWS).astype(BF16)
    fk = _head_rmsnorm(proj(OFF_CK, PROJ_WIDTH), ones_c) * fkg_ref[...]
    for hd in range(C_HEADS):
        slab = fk[:, (hd // 2) * BKV_WIDTH:(hd // 2 + 1) * BKV_WIDTH]
        low = slab if hd % 2 == 0 else pltpu.roll(slab, C_DH, axis=1)
        fk_ref[0, hd] = jnp.where(lane < C_DH, low, 0.0).astype(BF16)


def _inproj(x2d, g, w, wt, cos, sl, sh, cost, sint, qg, kg, fqg, fkg, bsz, seq, tb=512):
    n = bsz * seq
    nsb = seq // tb
    row = lambda i: (i, 0)
    const = lambda i: (0, 0)
    pos = lambda i: (i % nsb, 0)
    pos_t = lambda i: (0, i % nsb)
    heads = lambda i: (i // nsb, 0, i % nsb, 0)
    tokens_on_lanes = lambda i: (i // nsb, 0, i % nsb)
    return pl.pallas_call(
        functools.partial(_inproj_body, nsb),
        grid=(n // tb,),
        in_specs=[
            pl.BlockSpec((tb, D_MODEL), row),
            pl.BlockSpec((1, D_MODEL), const),
            pl.BlockSpec((D_MODEL, PROJ_WIDTH), const),
            pl.BlockSpec((T_ROWS, D_MODEL), const),
            pl.BlockSpec((tb, BKV_WIDTH), pos),
            pl.BlockSpec((tb, BKV_WIDTH), pos),
            pl.BlockSpec((tb, BKV_WIDTH), pos),
            pl.BlockSpec((ROT_HALF, tb), pos_t),
            pl.BlockSpec((ROT_HALF, tb), pos_t),
            pl.BlockSpec((B_DH, 1), const),
            pl.BlockSpec((1, BKV_WIDTH), const),
            pl.BlockSpec((C_DH, 1), const),
            pl.BlockSpec((1, C_WIDTH), const),
        ],
        out_specs=[
            pl.BlockSpec((tb, 4 * A_WIDTH), row),
            pl.BlockSpec((1, BQ_WIDTH, tb), tokens_on_lanes),
            pl.BlockSpec((1, 2 * B_GROUPS, tb, BKV_WIDTH), heads),
            pl.BlockSpec((1, 2 * BKV_WIDTH, tb), tokens_on_lanes),
            pl.BlockSpec((2, tb, BKV_WIDTH), lambda i: (0, i, 0)),
            pl.BlockSpec((tb, GATE_LANES), row),
            pl.BlockSpec((1, GATE_ROWS, tb), tokens_on_lanes),
            pl.BlockSpec((1, C_WIDTH, tb), tokens_on_lanes),
            pl.BlockSpec((1, C_HEADS, tb, BKV_WIDTH), heads),
            pl.BlockSpec((1, C_WIDTH, tb), tokens_on_lanes),
        ],
        out_shape=[
            jax.ShapeDtypeStruct((n, 4 * A_WIDTH), F32),
            jax.ShapeDtypeStruct((bsz, BQ_WIDTH, seq), BF16),
            jax.ShapeDtypeStruct((bsz, 2 * B_GROUPS, seq, BKV_WIDTH), BF16),
            jax.ShapeDtypeStruct((bsz, 2 * BKV_WIDTH, seq), BF16),
            jax.ShapeDtypeStruct((2, n, BKV_WIDTH), F32),
            jax.ShapeDtypeStruct((n, GATE_LANES), F32),
            jax.ShapeDtypeStruct((bsz, GATE_ROWS, seq), F32),
            jax.ShapeDtypeStruct((bsz, C_WIDTH, seq), BF16),
            jax.ShapeDtypeStruct((bsz, C_HEADS, seq, BKV_WIDTH), BF16),
            jax.ShapeDtypeStruct((bsz, C_WIDTH, seq), BF16),
        ],
        compiler_params=_params(("parallel",)),
        name="inproj",
    )(x2d, g, w, wt, cos, sl, sh, cost, sint, qg, kg, fqg, fkg)


HGRN_SUB = 16
HGRN_ROWS = 128
HGRN_BATCH = 4


def _hgrn_body(layer, hg_ref, lbl_ref, og_ref, o_ref, st_ref):
    @pl.when(pl.program_id(1) == 0)
    def _():
        st_ref[...] = jnp.zeros_like(st_ref)

    rows, width = HGRN_ROWS, A_WIDTH
    lbl = lbl_ref[...]
    e = jnp.exp(lbl - jnp.max(lbl, axis=0, keepdims=True))
    p = e / jnp.sum(e, axis=0, keepdims=True)
    lb = p[0:1] * 0.0
    for l in range(1, layer + 1):
        lb = lb + p[l:l + 1]

    r = _iota((rows, rows), 0)
    c = _iota((rows, rows), 1)
    tri = jnp.where((r // HGRN_SUB == c // HGRN_SUB) & (c <= r), 1.0, 0.0).astype(BF16)
    ones_h = _head_ones(width)
    tin = _iota((rows, width), 0) % HGRN_SUB
    head_of_lane = _iota((HGRN_SUB, width), 1) // HEAD_DIM

    def per_head(a):
        return jnp.concatenate([jnp.where(head_of_lane == hd, a, 0.0) for hd in range(A_HEADS)], axis=0).astype(BF16)

    for bb in range(hg_ref.shape[0]):
        q = hg_ref[bb, :, 0:width] * (A_DK ** -0.5)
        z = hg_ref[bb, :, width:2 * width]
        v = hg_ref[bb, :, 2 * width:3 * width]
        go = hg_ref[bb, :, 3 * width:4 * width]

        f = lb + (1.0 - lb) * _sigmoid(z)
        logf = jnp.log(jnp.maximum(f, TINY))
        k = (1.0 - lb) * _sigmoid(-z)

        gl = _dot_exact_rhs(tri, logf * LOG2E)

        o_band = jnp.zeros((rows, width), F32)
        for d in range(HGRN_SUB):
            ks, gs, vs = (k, gl, v) if d == 0 else (pltpu.roll(a, d, axis=0) for a in (k, gl, v))
            pr = q * ks * jnp.exp2(jnp.where(tin >= d, gl - gs, NEG_BIG))
            o_band = o_band + _dot(pr.astype(BF16), ones_h) * vs

        st = st_ref[bb]
        outs = []
        for i in range(rows // HGRN_SUB):
            sl_ = slice(i * HGRN_SUB, (i + 1) * HGRN_SUB)
            gi, qi, ki, vi = gl[sl_], q[sl_], k[sl_], v[sl_]
            outs.append(_dot_nt((qi * jnp.exp2(gi)).astype(BF16), st.astype(BF16)))
            glast = gi[HGRN_SUB - 1:HGRN_SUB]
            kdec = ki * jnp.exp2(glast - gi)
            st = st * jnp.exp2(glast) + _dot_tn(per_head(vi), per_head(kdec))
        st_ref[bb] = st
        o = jnp.concatenate(outs, axis=0) + o_band

        o = _head_rmsnorm(o, ones_h) * og_ref[...]
        gate = go * _sigmoid(go)
        o_ref[bb] = (o * gate).astype(o_ref.dtype)


def _hgrn(hg3, lb_logits, og, layer):
    bsz, seq, _ = hg3.shape
    nb = HGRN_BATCH if bsz % HGRN_BATCH == 0 else 1
    return pl.pallas_call(
        functools.partial(_hgrn_body, layer),
        grid=(bsz // nb, seq // HGRN_ROWS),
        in_specs=[
            pl.BlockSpec((nb, HGRN_ROWS, 4 * A_WIDTH), lambda b, i: (b, i, 0)),
            pl.BlockSpec((DEPTH, A_WIDTH), lambda b, i: (0, 0)),
            pl.BlockSpec((1, A_WIDTH), lambda b, i: (0, 0)),
        ],
        out_specs=pl.BlockSpec((nb, HGRN_ROWS, A_WIDTH), lambda b, i: (b, i, 0)),
        out_shape=jax.ShapeDtypeStruct((bsz, seq, A_WIDTH), BF16),
        scratch_shapes=[pltpu.VMEM((nb, A_WIDTH, A_WIDTH), F32)],
        compiler_params=_params(("parallel", "arbitrary")),
        name="hgrn2",
    )(hg3, lb_logits, og)


FOXC_ROWS = 512


FOX_AUG = 16


def _foxc_body(sc_ref, gt_ref, fb_ref, ck_ref, cq_ref, carry_ref):
    @pl.when(pl.program_id(1) == 0)
    def _():
        carry_ref[...] = jnp.zeros_like(carry_ref)

    rows = FOXC_ROWS
    y = gt_ref[0] + fb_ref[...]
    lf = jnp.minimum(y, 0.0) - jnp.log(1.0 + jnp.exp(-jnp.abs(y)))
    tri = jnp.where(_iota((rows, rows), 1) <= _iota((rows, rows), 0), 1.0, 0.0).astype(BF16)
    cc = _dot_exact_rhs(tri, lf) + carry_ref[...]
    carry_ref[...] = cc[rows - 1:rows]
    parts = jnp.concatenate(_split3(cc * LOG2E), axis=1)

    pr = _iota((3 * GATE_LANES, BKV_WIDTH), 0)
    pc = _iota((3 * GATE_LANES, BKV_WIDTH), 1)
    sr = _iota((FOX_AUG, 3 * GATE_LANES), 0)
    sc = _iota((FOX_AUG, 3 * GATE_LANES), 1)
    k_lane = _iota((rows, BKV_WIDTH), 1)
    ones_k = jnp.where(((k_lane >= C_DH) & (k_lane < C_DH + 3)) | (k_lane == C_DH + 6), 1.0, 0.0)
    q_row = _iota((FOX_AUG, rows), 0)
    ones_q = jnp.where((q_row >= 3) & (q_row < 6), 1.0, 0.0)
    ones_q = jnp.where(q_row == 6, -sc_ref[0], ones_q)
    for hd in range(C_HEADS):
        src = FOX_GATE_LANE0 + hd
        place_k = jnp.where((pr % GATE_LANES == src) & (pc == C_DH + 3 + pr // GATE_LANES), -1.0, 0.0).astype(BF16)
        place_q = jnp.where((sc % GATE_LANES == src) & (sr == sc // GATE_LANES), 1.0, 0.0).astype(BF16)
        ck_ref[0, hd] = (_dot(parts, place_k) + ones_k).astype(BF16)
        cq_ref[0, hd] = (_dot_nt(place_q, parts) + ones_q).astype(BF16)


def _foxc(bound, gates3, fb_row):
    bsz, seq, _ = gates3.shape
    return pl.pallas_call(
        _foxc_body,
        grid=(bsz, seq // FOXC_ROWS),
        in_specs=[
            pl.BlockSpec(memory_space=pltpu.SMEM),
            pl.BlockSpec((1, FOXC_ROWS, GATE_LANES), lambda b, i: (b, i, 0)),
            pl.BlockSpec((1, GATE_LANES), lambda b, i: (0, 0)),
        ],
        out_specs=[
            pl.BlockSpec((1, C_HEADS, FOXC_ROWS, BKV_WIDTH), lambda b, i: (b, 0, i, 0)),
            pl.BlockSpec((1, C_HEADS, FOX_AUG, FOXC_ROWS), lambda b, i: (b, 0, 0, i)),
        ],
        out_shape=[
            jax.ShapeDtypeStruct((bsz, C_HEADS, seq, BKV_WIDTH), BF16),
            jax.ShapeDtypeStruct((bsz, C_HEADS, FOX_AUG, seq), BF16),
        ],
        scratch_shapes=[pltpu.VMEM((1, GATE_LANES), F32)],
        compiler_params=_params(("parallel", "arbitrary")),
        name="fox_cumgate",
    )(bound, gates3, fb_row)


FOX_BLK = 512


def _fox_body(qi_ref, kj_ref, sc_ref, qt_ref, cq_ref, k_ref, ck_ref, vt_ref, o_ref, m_ref, acc_ref):
    qi, kj = qi_ref[pl.program_id(1)], kj_ref[pl.program_id(1)]
    blk = FOX_BLK
    use_fixed = sc_ref[1] > 0.5

    @pl.when(kj == 0)
    def _():
        m_ref[...] = jnp.full_like(m_ref, NEG_BIG)
        acc_ref[...] = jnp.zeros_like(acc_ref)

    def update(diagonal, fixed):
        pad = jnp.zeros((BKV_WIDTH - C_DH - FOX_AUG, blk), BF16)
        for hd in range(C_HEADS):
            rows = slice(hd * C_DH, (hd + 1) * C_DH)
            q_aug = jnp.concatenate([qt_ref[0, rows, :], cq_ref[0, hd], pad], axis=0)
            s = _dot(k_ref[0, hd] + ck_ref[0, hd], q_aug)
            if diagonal:
                s = jnp.where(_iota((blk, blk), 0) <= _iota((blk, blk), 1), s, NEG_BIG)
            carry = (acc_ref[hd],) if fixed else (m_ref[hd], acc_ref[hd])
            carry = _attn_step(s, _with_sum_rows(vt_ref[0, rows, :]), carry, fixed)
            acc_ref[hd] = carry[-1]
            if not fixed:
                m_ref[hd] = carry[0]
            if diagonal:
                o_ref[0, rows, :] = _attn_finish(carry).astype(o_ref.dtype)

    for fixed in (True, False):
        pick = use_fixed if fixed else jnp.logical_not(use_fixed)
        pl.when(pick & (kj < qi))(functools.partial(update, False, fixed))
        pl.when(pick & (kj == qi))(functools.partial(update, True, fixed))


def _fox(bound, fqt, cq, fk, ck, fvt):
    bsz, _, seq = fqt.shape
    nb = seq // FOX_BLK
    pairs = [(i, j) for i in range(nb) for j in range(i + 1)]
    qi = jnp.asarray([p[0] for p in pairs], jnp.int32)
    kj = jnp.asarray([p[1] for p in pairs], jnp.int32)
    q_lanes = lambda b, p, qi, kj: (b, 0, qi[p])
    k_rows = lambda b, p, qi, kj: (b, 0, kj[p], 0)
    grid_spec = pltpu.PrefetchScalarGridSpec(
        num_scalar_prefetch=2,
        grid=(bsz, len(pairs)),
        in_specs=[
            pl.BlockSpec(memory_space=pltpu.SMEM),
            pl.BlockSpec((1, C_WIDTH, FOX_BLK), q_lanes),
            pl.BlockSpec((1, C_HEADS, FOX_AUG, FOX_BLK), lambda b, p, qi, kj: (b, 0, 0, qi[p])),
            pl.BlockSpec((1, C_HEADS, FOX_BLK, BKV_WIDTH), k_rows),
            pl.BlockSpec((1, C_HEADS, FOX_BLK, BKV_WIDTH), k_rows),
            pl.BlockSpec((1, C_WIDTH, FOX_BLK), lambda b, p, qi, kj: (b, 0, kj[p])),
        ],
        out_specs=pl.BlockSpec((1, C_WIDTH, FOX_BLK), q_lanes),
        scratch_shapes=[
            pltpu.VMEM((C_HEADS, 1, FOX_BLK), F32),
            pltpu.VMEM((C_HEADS, C_DH + SUM_ROWS, FOX_BLK), F32),
        ],
    )
    return pl.pallas_call(
        _fox_body,
        grid_spec=grid_spec,
        out_shape=jax.ShapeDtypeStruct((bsz, C_WIDTH, seq), BF16),
        compiler_params=_params(("parallel", "arbitrary")),
        name="fox_attn",
    )(qi, kj, bound, fqt, cq, fk, ck, fvt)


def _cmp_body(x_ref, pos_ref, w1_ref, w2k_ref, w2vt_ref, kg_ref, cos_ref, sl_ref, sh_ref, kc_ref, vct_ref):
    n_blk = x_ref.shape[2] // CMP_STRIDE
    hw = B_GROUPS * CMP_HIDDEN
    first = [jnp.zeros((n_blk, hw), F32) for _ in range(2)]
    second = [jnp.zeros((n_blk, hw), F32) for _ in range(2)]
    for l in range(CMP_STRIDE):
        for which in range(2):
            xw = x_ref[which, 0, pl.ds(l, n_blk, stride=CMP_STRIDE), :]
            a = (xw + pos_ref[which, l:l + 1, :]).astype(BF16)
            b = (xw + pos_ref[which, CMP_STRIDE + l:CMP_STRIDE + l + 1, :]).astype(BF16)
            first[which] = first[which] + _dot(a, w1_ref[which, l])
            second[which] = second[which] + _dot(b, w1_ref[which, CMP_STRIDE + l])
    hids = []
    for which in range(2):
        hid = first[which] + pltpu.roll(second[which], n_blk - 1, axis=0)
        hid = 0.5 * hid * (1.0 + jnp.tanh(math.sqrt(2.0 / math.pi) * (hid + 0.044715 * hid * hid * hid)))
        hids.append(hid.astype(BF16))
    kc = _dot(hids[0], w2k_ref[...])
    kc = _head_rmsnorm(kc, _head_ones(BKV_WIDTH)) * kg_ref[...]
    kc = _rope(kc, cos_ref[...], sl_ref[...], sh_ref[...])
    vct = _dot_nt(w2vt_ref[...], hids[1])
    lane = _iota((n_blk, BKV_WIDTH), 1)
    for g in range(B_GROUPS):
        low = kc if g == 0 else pltpu.roll(kc, B_DH, axis=1)
        kc_ref[0, g] = jnp.where(lane < B_DH, low, jnp.where(lane == B_DH, 1.0, 0.0)).astype(BF16)
        vct_ref[0, g] = vct[g * B_DH:(g + 1) * B_DH].astype(BF16)


def _compress(cmp4, pos_e, w1_e, w2k_e, w2vt_e, kg, cos, sl, sh):
    _, bsz, seq, _ = cmp4.shape
    n_blk = seq // CMP_STRIDE
    c2 = lambda b: (0, 0)
    c3 = lambda b: (0, 0, 0)
    c4 = lambda b: (0, 0, 0, 0)
    return pl.pallas_call(
        _cmp_body,
        grid=(bsz,),
        in_specs=[
            pl.BlockSpec((2, 1, seq, BKV_WIDTH), lambda b: (0, b, 0, 0)),
            pl.BlockSpec((2, CMP_BLOCK, BKV_WIDTH), c3),
            pl.BlockSpec((2, CMP_BLOCK, BKV_WIDTH, B_GROUPS * CMP_HIDDEN), c4),
            pl.BlockSpec((B_GROUPS * CMP_HIDDEN, BKV_WIDTH), c2),
            pl.BlockSpec((BKV_WIDTH, B_GROUPS * CMP_HIDDEN), c2),
            pl.BlockSpec((1, BKV_WIDTH), c2),
            pl.BlockSpec((n_blk, BKV_WIDTH), c2),
            pl.BlockSpec((n_blk, BKV_WIDTH), c2),
            pl.BlockSpec((n_blk, BKV_WIDTH), c2),
        ],
        out_specs=[
            pl.BlockSpec((1, B_GROUPS, n_blk, BKV_WIDTH), lambda b: (b, 0, 0, 0)),
            pl.BlockSpec((1, B_GROUPS, B_DH, n_blk), lambda b: (b, 0, 0, 0)),
        ],
        out_shape=[
            jax.ShapeDtypeStruct((bsz, B_GROUPS, n_blk, BKV_WIDTH), BF16),
            jax.ShapeDtypeStruct((bsz, B_GROUPS, B_DH, n_blk), BF16),
        ],
        compiler_params=_params(("parallel",)),
        name="nsa_compress",
    )(cmp4, pos_e, w1_e, w2k_e, w2vt_e, kg, cos, sl, sh)


NSA_Q = 256
NSA_KB = 256
assert NSA_KB % NSA_Q == 0, "a query block must sit inside one key block (single diagonal step)"
SEL_UNROLL = 4


def _nsa_body(sc_ref, qt_ref, kc_ref, vct_ref, ka_ref, vt_ref, gt_ref, o_ref):
    tq = NSA_Q
    lanes = B_HPG * tq
    n_cmp = kc_ref.shape[2]
    n_sel = ka_ref.shape[2] // SEL_BLOCK
    q0 = pl.program_id(1) * tq
    t_lane = q0 + _iota((1, lanes), 1) % tq
    gates = _sigmoid(gt_ref[0])

    cn = _iota((n_sel, n_cmp), 1)
    cj = _iota((n_sel, n_cmp), 0)
    overlap_t = jnp.where((cn * CMP_STRIDE <= cj * SEL_BLOCK + SEL_BLOCK - 1)
                          & (cn * CMP_STRIDE + CMP_BLOCK - 1 >= cj * SEL_BLOCK), 1.0, 0.0).astype(BF16)
    jt = _iota((n_sel, tq), 0)
    row8 = _iota((8, tq), 0)
    tt = q0 + _iota((n_sel, tq), 1)
    cur = tt // SEL_BLOCK
    cmp_end = _iota((n_cmp, lanes), 0) * CMP_STRIDE + CMP_BLOCK - 1
    key_kb = _iota((NSA_KB, lanes), 0)
    kb_diag = q0 // NSA_KB
    offset = sc_ref[0]
    neg_offset = jnp.full((B_DH, lanes), -offset, F32).astype(BF16)

    head_rows = [[slice((g * B_HPG + hd) * B_DH, (g * B_HPG + hd + 1) * B_DH) for hd in range(B_HPG)]
                 for g in range(B_GROUPS)]
    def prepare(g, fixed):
        qt = jnp.concatenate([qt_ref[0, r, :] for r in head_rows[g]], axis=1)
        q_win = jnp.concatenate([qt, neg_offset], axis=0)

        mask_c = cmp_end <= t_lane
        s_c = jnp.where(mask_c, _dot(kc_ref[0, g], q_win), NEG_BIG)
        if fixed:
            p_c = jnp.exp2(s_c)
        else:
            p_c = jnp.where(mask_c, jnp.exp2(s_c - jnp.max(s_c, axis=0, keepdims=True)), 0.0)
        p_c = p_c * (1.0 / jnp.maximum(jnp.sum(p_c, axis=0, keepdims=True), TINY))
        o_cmp = _dot(vct_ref[0, g], p_c.astype(BF16))

        p_sum = p_c[:, 0:tq]
        for hd in range(1, B_HPG):
            p_sum = p_sum + p_c[:, hd * tq:(hd + 1) * tq]
        p_hi = p_sum.astype(BF16)
        p_lo = (p_sum - p_hi.astype(F32)).astype(BF16)
        imp = _dot(overlap_t, p_hi) + _dot(overlap_t, p_lo)
        imp = jnp.where((jt == 0) | (jt == cur) | (jt == cur - 1), POS_BIG, imp)
        imp = jnp.where(jt * SEL_BLOCK > tt, NEG_BIG, imp)
        tiles = [imp[r:r + 8] for r in range(0, n_sel, 8)]
        ranks = [jnp.zeros((8, tq), F32) for _ in tiles]
        for j2 in range(n_sel):
            other = imp[j2:j2 + 1, :]
            for ti, tile in enumerate(tiles):
                if ti > j2 // 8:
                    beats = other >= tile
                elif ti < j2 // 8:
                    beats = other > tile
                else:
                    beats = (other > tile) | ((other == tile) & (j2 % 8 < row8))
                ranks[ti] = ranks[ti] + jnp.where(beats, 1.0, 0.0)
        rank = jnp.concatenate(ranks, axis=0)
        bias = jnp.where((rank < SEL_TOPK) & (imp > NEG_BIG * 0.5), -offset, NEG_BIG).astype(BF16)
        if n_sel < BKV_WIDTH - B_DH:
            bias = jnp.concatenate([bias, jnp.zeros((BKV_WIDTH - B_DH - n_sel, tq), BF16)], axis=0)
        q_sel = jnp.concatenate([qt, jnp.concatenate([bias] * B_HPG, axis=1)], axis=0)
        return o_cmp, q_sel, q_win

    def scores(slot, g, start, n_keys, q_aug):
        st = pl.multiple_of(start, NSA_KB)
        s = _dot(ka_ref[0, slot * B_GROUPS + g, pl.ds(st, n_keys), :], q_aug)
        vt = vt_ref[0, (slot * B_GROUPS + g) * B_DH:(slot * B_GROUPS + g + 1) * B_DH, pl.ds(st, n_keys)]
        return s, _with_sum_rows(vt)

    def gate_row(branch, g):
        base = branch * B_HEADS + g * B_HPG
        return jnp.concatenate([gates[base + hd:base + hd + 1, :] for hd in range(B_HPG)], axis=1)

    def attend(fixed):
        o_cmp, q_sel, q_win = zip(*(prepare(g, fixed) for g in range(B_GROUPS)))
        init = (_attn_init(lanes, fixed),) * B_GROUPS

        def sel_blocks(kb, carry, n_blocks):
            carry = list(carry)
            for u in range(n_blocks):
                for g in range(B_GROUPS):
                    s, vt = scores(0, g, (kb + u) * NSA_KB, NSA_KB, q_sel[g])
                    carry[g] = _attn_step(s, vt, carry[g], fixed)
            return tuple(carry)

        carry = lax.fori_loop(0, kb_diag // SEL_UNROLL, lambda i, c: sel_blocks(SEL_UNROLL * i, c, SEL_UNROLL), init)
        carry = lax.fori_loop(kb_diag - kb_diag % SEL_UNROLL, kb_diag, lambda kb, c: sel_blocks(kb, c, 1), carry)
        o_sel = []
        for g in range(B_GROUPS):
            s, vt = scores(0, g, q0, NSA_KB, q_sel[g])
            s = jnp.where(q0 + key_kb <= t_lane, s, NEG_BIG)
            o_sel.append(_attn_finish(_attn_step(s, vt, carry[g], fixed)))

        win_start = jnp.maximum(q0 - WINDOW, 0)
        kpos = win_start + _iota((WINDOW + NSA_Q, lanes), 0)
        mask_w = (kpos <= t_lane) & (kpos > t_lane - WINDOW)
        o_win = []
        for g in range(B_GROUPS):
            s, vt = scores(1, g, win_start, WINDOW + NSA_Q, q_win[g])
            o_win.append(_attn_finish(_attn_step(jnp.where(mask_w, s, NEG_BIG), vt, init[g], fixed)))

        for g in range(B_GROUPS):
            o = gate_row(0, g) * o_cmp[g] + gate_row(1, g) * o_sel[g] + gate_row(2, g) * o_win[g]
            for hd in range(B_HPG):
                o_ref[0, head_rows[g][hd], :] = o[:, hd * tq:(hd + 1) * tq].astype(o_ref.dtype)

    use_fixed = sc_ref[1] > 0.5
    pl.when(use_fixed)(functools.partial(attend, True))
    pl.when(jnp.logical_not(use_fixed))(functools.partial(attend, False))


def _nsa(bound, qt, kc, vct, ka, vt, gates_t):
    bsz, _, seq = qt.shape
    n_cmp = kc.shape[2]
    assert seq // SEL_BLOCK <= BKV_WIDTH - B_DH, "one selection-block id per spare key lane"
    assert seq >= WINDOW + NSA_Q and WINDOW % NSA_KB == 0
    whole = lambda b, i: (b, 0, 0, 0)
    return pl.pallas_call(
        _nsa_body,
        grid=(bsz, seq // NSA_Q),
        in_specs=[
            pl.BlockSpec(memory_space=pltpu.SMEM),
            pl.BlockSpec((1, BQ_WIDTH, NSA_Q), lambda b, i: (b, 0, i)),
            pl.BlockSpec((1, B_GROUPS, n_cmp, BKV_WIDTH), whole),
            pl.BlockSpec((1, B_GROUPS, B_DH, n_cmp), whole),
            pl.BlockSpec((1, 2 * B_GROUPS, seq, BKV_WIDTH), whole),
            pl.BlockSpec((1, 2 * BKV_WIDTH, seq), lambda b, i: (b, 0, 0)),
            pl.BlockSpec((1, GATE_ROWS, NSA_Q), lambda b, i: (b, 0, i)),
        ],
        out_specs=pl.BlockSpec((1, BQ_WIDTH, NSA_Q), lambda b, i: (b, 0, i)),
        out_shape=jax.ShapeDtypeStruct((bsz, BQ_WIDTH, seq), BF16),
        compiler_params=_params(("parallel", "arbitrary")),
        name="nsa_attn",
    )(bound, qt, kc, vct, ka, vt, gates_t)


FFN_ROWS = 1024
FFN_COLS = 1024


def _mix_ffn_body(x_ref, oa_ref, ob_ref, oc_ref, wo_ref, g_ref, wu_ref, wd_ref, o_ref, h_ref, acc_ref):
    j = pl.program_id(1)

    @pl.when(j == 0)
    def _():
        x = x_ref[...] + _dot(oa_ref[...], wo_ref[0:A_WIDTH, :])
        x = x + _dot_tn(ob_ref[0], wo_ref[A_WIDTH:A_WIDTH + BQ_WIDTH, :])
        x = x + _dot_tn(oc_ref[0], wo_ref[A_WIDTH + BQ_WIDTH:, :])
        var = jnp.mean(x * x, axis=-1, keepdims=True)
        h_ref[...] = (x * lax.rsqrt(var + EPS) * g_ref[...]).astype(BF16)
        acc_ref[...] = x

    u = jnp.maximum(_dot(h_ref[...], wu_ref[...]), 0.0)
    acc_ref[...] += _dot((u * u).astype(BF16), wd_ref[...])

    @pl.when(j == pl.num_programs(1) - 1)
    def _():
        o_ref[...] = acc_ref[...]


def _mix_ffn(x2d, oa2d, ob, oc, w_o, g, w_up, w_down, bsz, seq):
    n = bsz * seq
    nsb = seq // FFN_ROWS
    row = lambda i, j: (i, 0)
    const = lambda i, j: (0, 0)
    tokens_on_lanes = lambda i, j: (i // nsb, 0, i % nsb)
    return pl.pallas_call(
        _mix_ffn_body,
        grid=(n // FFN_ROWS, D_FF // FFN_COLS),
        in_specs=[
            pl.BlockSpec((FFN_ROWS, D_MODEL), row),
            pl.BlockSpec((FFN_ROWS, A_WIDTH), row),
            pl.BlockSpec((1, BQ_WIDTH, FFN_ROWS), tokens_on_lanes),
            pl.BlockSpec((1, C_WIDTH, FFN_ROWS), tokens_on_lanes),
            pl.BlockSpec((D_MODEL, D_MODEL), const),
            pl.BlockSpec((1, D_MODEL), const),
            pl.BlockSpec((D_MODEL, FFN_COLS), lambda i, j: (0, j)),
            pl.BlockSpec((FFN_COLS, D_MODEL), lambda i, j: (j, 0)),
        ],
        out_specs=pl.BlockSpec((FFN_ROWS, D_MODEL), row),
        out_shape=jax.ShapeDtypeStruct((n, D_MODEL), F32),
        scratch_shapes=[pltpu.VMEM((FFN_ROWS, D_MODEL), BF16), pltpu.VMEM((FFN_ROWS, D_MODEL), F32)],
        compiler_params=_params(("parallel", "arbitrary")),
        name="mix_ffn",
    )(x2d, oa2d, ob, oc, w_o, g, w_up, w_down)


def _rope_tables(pos, n_heads):
    inv_freq = jnp.power(jnp.float32(ROPE_THETA), -jnp.arange(0, ROT_DIM, 2, dtype=F32) / ROT_DIM)
    ang = pos.astype(F32)[:, None] * inv_freq[None, :]
    cos, sin = jnp.cos(ang), jnp.sin(ang)
    n = pos.shape[0]
    pad = jnp.zeros((n, HEAD_DIM - ROT_DIM), F32)
    zero = jnp.zeros((n, ROT_HALF), F32)
    c = jnp.concatenate([cos, cos, pad + 1.0], axis=1)
    s_lo = jnp.concatenate([zero, sin, pad], axis=1)
    s_hi = jnp.concatenate([-sin, zero, pad], axis=1)
    return tuple(jnp.tile(t, (1, n_heads)) for t in (c, s_lo, s_hi))


def _rope_tables_t(pos):
    inv_freq = jnp.power(jnp.float32(ROPE_THETA), -jnp.arange(0, ROT_DIM, 2, dtype=F32) / ROT_DIM)
    ang = pos.astype(F32)[:, None] * inv_freq[None, :]
    return jnp.cos(ang).T, jnp.sin(ang).T


def _pack_w_in(w):
    w = w.astype(BF16)
    cols = lambda lo, width: w[:, lo:lo + width]
    gates = jnp.concatenate([cols(SRC_BG, N_GATES), cols(SRC_CF, C_HEADS)], axis=1)
    pad = lambda width: jnp.zeros((D_MODEL, width - N_GATES - C_HEADS), BF16)
    w_rows = jnp.concatenate([cols(SRC_A, 4 * A_WIDTH), cols(SRC_KC, 2 * BKV_WIDTH), cols(SRC_KS, BKV_WIDTH),
                              cols(SRC_KW, BKV_WIDTH), gates, pad(GATE_LANES), cols(SRC_CK, C_WIDTH)], axis=1)
    w_lanes = jnp.concatenate([cols(SRC_BQ, BQ_WIDTH), cols(SRC_VS, BKV_WIDTH), cols(SRC_VW, BKV_WIDTH),
                               gates, pad(GATE_ROWS), cols(SRC_CQ, C_WIDTH), cols(SRC_CV, C_WIDTH)], axis=1)
    return w_rows, w_lanes.T


def _expand_groups(w):
    z = jnp.zeros_like(w)
    top = jnp.concatenate([w, z], axis=-1)
    bot = jnp.concatenate([z, w], axis=-1)
    return jnp.concatenate([top, bot], axis=-2)


def kernel(x, norm1_g, w_in, hgrn_lb_logits, hgrn_onorm_g, nsa_qn_g, nsa_kn_g, nsa_cmp_pos, nsa_cmp_w1,
           nsa_cmp_w2, fox_qn_g, fox_kn_g, fox_fb, w_o, norm2_g, w_up, w_down):
    bsz, seq, _ = x.shape
    n = bsz * seq
    n_blk = seq // CMP_STRIDE
    cos, s_lo, s_hi = _rope_tables(jnp.arange(seq), B_GROUPS)
    cos_t, sin_t = _rope_tables_t(jnp.arange(seq))
    ccos, cs_lo, cs_hi = _rope_tables(jnp.arange(n_blk) * CMP_STRIDE + CMP_BLOCK - 1, B_GROUPS)

    x2d = x.reshape(n, D_MODEL)
    for layer in range(DEPTH):
        row = lambda v, reps: jnp.tile(v[layer].astype(F32), reps)[None, :]
        w_rows, w_lanes = _pack_w_in(w_in[layer])
        hg, qt, ka, vt, cmp2d, gates, gates_t, fqt, fk, fvt = _inproj(
            x2d, norm1_g[layer][None, :], w_rows, w_lanes, cos, s_lo, s_hi, cos_t, sin_t,
            nsa_qn_g[layer].astype(F32)[:, None], row(nsa_kn_g, B_GROUPS), fox_qn_g[layer].astype(F32)[:, None],
            row(fox_kn_g, C_HEADS), bsz, seq)

        o_a = _hgrn(hg.reshape(bsz, seq, 4 * A_WIDTH), hgrn_lb_logits.astype(F32),
                    row(hgrn_onorm_g, A_HEADS), layer)

        gates3 = gates.reshape(bsz, seq, GATE_LANES)
        fb_row = jnp.zeros((1, GATE_LANES), F32).at[0, FOX_GATE_LANE0:FOX_GATE_LANE0 + C_HEADS].set(
            fox_fb[layer].astype(F32))
        fox_bound = _score_bound(fox_qn_g[layer], fox_kn_g[layer])
        ck, cq = _foxc(fox_bound, gates3, fb_row)
        o_c = _fox(fox_bound, fqt, cq, fk, ck, fvt)

        w1 = nsa_cmp_w1[layer].reshape(2, CMP_BLOCK, B_DH, CMP_HIDDEN)
        w1_e = _expand_groups(w1).astype(BF16)
        w2_e = _expand_groups(nsa_cmp_w2[layer]).astype(BF16)
        pos_e = jnp.tile(nsa_cmp_pos[layer].astype(F32), (1, 1, B_GROUPS))
        kc, vct = _compress(cmp2d.reshape(2, bsz, seq, BKV_WIDTH), pos_e, w1_e, w2_e[0], w2_e[1].T,
                            row(nsa_kn_g, B_GROUPS), ccos, cs_lo, cs_hi)
        o_b = _nsa(_score_bound(nsa_qn_g[layer], nsa_kn_g[layer]), qt, kc, vct, ka, vt, gates_t)

        x2d = _mix_ffn(x2d, o_a.reshape(n, A_WIDTH), o_b, o_c, w_o[layer].astype(BF16), norm2_g[layer][None, :],
                       w_up[layer].astype(BF16), w_down[layer].astype(BF16), bsz, seq)
    return x2d.reshape(bsz, seq, D_MODEL)
```

```python
import functools
import math

import numpy as np
import jax
import jax.numpy as jnp
from jax import lax
from jax.experimental import pallas as pl
from jax.experimental.pallas import tpu as pltpu

F32 = jnp.float32
BF16 = jnp.bfloat16

D_MODEL = 1024
DEPTH = 2
A_HEADS, A_DK, A_DV = 4, 64, 64
B_HEADS, B_GROUPS, B_DH = 8, 2, 64
B_HPG = B_HEADS // B_GROUPS
C_HEADS, C_DH = 4, 64
HEAD_DIM = 64
D_FF = 4 * D_MODEL
ROPE_THETA = 500000.0
ROT_DIM = B_DH // 4
ROT_HALF = ROT_DIM // 2
EPS = 1e-6
NEG_BIG = -1e30
POS_BIG = 1e30
TINY = 1e-30
CMP_BLOCK, CMP_STRIDE, CMP_HIDDEN = 32, 16, 128
SEL_BLOCK, SEL_TOPK, WINDOW = 64, 16, 512

A_WIDTH = A_HEADS * A_DK
BQ_WIDTH = B_HEADS * B_DH
BKV_WIDTH = B_GROUPS * B_DH
C_WIDTH = C_HEADS * C_DH
N_GATES = 3 * B_HEADS
GATE_LANES = 128
FOX_GATE_LANE0 = N_GATES

SRC_A = 0
SRC_BQ = 4 * A_WIDTH
SRC_KC = SRC_BQ + BQ_WIDTH
SRC_VC = SRC_KC + BKV_WIDTH
SRC_KS = SRC_VC + BKV_WIDTH
SRC_VS = SRC_KS + BKV_WIDTH
SRC_KW = SRC_VS + BKV_WIDTH
SRC_VW = SRC_KW + BKV_WIDTH
SRC_BG = SRC_VW + BKV_WIDTH
SRC_CQ = SRC_BG + N_GATES
SRC_CK = SRC_CQ + C_WIDTH
SRC_CV = SRC_CK + C_WIDTH
SRC_CF = SRC_CV + C_WIDTH

OFF_A = 0
OFF_KC = 4 * A_WIDTH
OFF_VC = OFF_KC + BKV_WIDTH
OFF_KS = OFF_VC + BKV_WIDTH
OFF_KW = OFF_KS + BKV_WIDTH
OFF_GATE = OFF_KW + BKV_WIDTH
OFF_CK = OFF_GATE + GATE_LANES
PROJ_WIDTH = OFF_CK + C_WIDTH

VMEM_LIMIT = 56 * 1024 * 1024


def _params(sem):
    return pltpu.CompilerParams(dimension_semantics=sem, vmem_limit_bytes=VMEM_LIMIT)


def _dot(a, b):
    return jnp.dot(a, b, preferred_element_type=F32)


def _dot_nt(a, b):
    return lax.dot_general(a, b, (((1,), (1,)), ((), ())), preferred_element_type=F32)


def _dot_tn(a, b):
    return lax.dot_general(a, b, (((0,), (0,)), ((), ())), preferred_element_type=F32)


def _split3(x):
    a = x.astype(BF16)
    r = x - a.astype(F32)
    b = r.astype(BF16)
    c = (r - b.astype(F32)).astype(BF16)
    return a, b, c


def _dot_exact_rhs(m01, x):
    a, b, c = _split3(x)
    return _dot(m01, a) + _dot(m01, b) + _dot(m01, c)


def _iota(shape, dim):
    return lax.broadcasted_iota(jnp.int32, shape, dim)


def _head_ones(width):
    r = _iota((width, width), 0) // HEAD_DIM
    c = _iota((width, width), 1) // HEAD_DIM
    return jnp.where(r == c, 1.0, 0.0).astype(BF16)


def _head_rmsnorm(x, head_ones):
    x2 = x * x
    hi = x2.astype(BF16)
    lo = (x2 - hi.astype(F32)).astype(BF16)
    ms = (_dot(hi, head_ones) + _dot(lo, head_ones)) * (1.0 / HEAD_DIM)
    return x * lax.rsqrt(ms + EPS)


def _rope(x, cos, sin_lo, sin_hi):
    n = x.shape[-1]
    return (x * cos + pltpu.roll(x, ROT_HALF, axis=1) * sin_lo
            + pltpu.roll(x, n - ROT_HALF, axis=1) * sin_hi)


def _sigmoid(x):
    return 1.0 / (1.0 + jnp.exp(-x))


LOG2E = 1.4426950408889634
SUM_ROWS = 16
FIXED_OFFSET_LIMIT = 40.0


def _score_bound(gain_q, gain_k):
    bound = (1.02 * LOG2E * HEAD_DIM ** 0.5) * jnp.max(jnp.abs(gain_q)) * jnp.max(jnp.abs(gain_k))
    fixed = bound < FIXED_OFFSET_LIMIT
    return jnp.stack([jnp.where(fixed, bound, 0.0), jnp.where(fixed, 1.0, 0.0)]).astype(F32)


def _with_sum_rows(vt):
    return jnp.concatenate([vt, jnp.ones((SUM_ROWS, vt.shape[1]), vt.dtype)], axis=0)


def _attn_init(lanes, fixed):
    acc = jnp.zeros((HEAD_DIM + SUM_ROWS, lanes), F32)
    return (acc,) if fixed else (jnp.full((1, lanes), NEG_BIG, F32), acc)


def _attn_step(s, vt_aug, carry, fixed):
    if fixed:
        return (carry[0] + _dot(vt_aug, jnp.exp2(s).astype(BF16)),)
    m_old, acc = carry
    m_new = jnp.maximum(m_old, jnp.max(s, axis=0, keepdims=True))
    p = jnp.exp2(s - m_new)
    return m_new, jnp.exp2(m_old - m_new) * acc + _dot(vt_aug, p.astype(BF16))


def _attn_finish(carry):
    acc = carry[-1]
    return acc[:HEAD_DIM] / jnp.maximum(acc[HEAD_DIM:HEAD_DIM + 1], TINY)


TOFF_Q = 0
TOFF_V = BQ_WIDTH
TOFF_GATE = TOFF_V + 2 * BKV_WIDTH
GATE_ROWS = 32
TOFF_FQ = TOFF_GATE + GATE_ROWS
TOFF_FV = TOFF_FQ + C_WIDTH
T_ROWS = TOFF_FV + C_WIDTH


def _inproj_body(seq_blocks, x_ref, g_ref, w_ref, wt_ref, cos_ref, sl_ref, sh_ref, cost_ref, sint_ref, qg_ref,
                 kg_ref, fqg_ref, fkg_ref, hg_ref, qt_ref, ka_ref, vt_ref, cmp_ref, gt_ref, gtt_ref,
                 fqt_ref, fk_ref, fvt_ref):
    tb = x_ref.shape[0]
    x = x_ref[...]
    var = jnp.mean(x * x, axis=-1, keepdims=True)
    h = (x * lax.rsqrt(var + EPS) * g_ref[...]).astype(BF16)

    rows_proj = _dot(h, w_ref[...])

    def proj(lo, hi):
        return rows_proj[:, lo:hi]

    def proj_t(lo, hi):
        return _dot_nt(wt_ref[lo:hi, :], h)

    hg_ref[...] = proj(OFF_A, OFF_KC)

    q = proj_t(TOFF_Q, TOFF_V).reshape(B_HEADS, B_DH, tb)
    q = q * lax.rsqrt(jnp.mean(q * q, axis=1, keepdims=True) + EPS) * qg_ref[...]
    cost, sint = cost_ref[...], sint_ref[...]
    x1, x2 = q[:, 0:ROT_HALF], q[:, ROT_HALF:ROT_DIM]
    q = jnp.concatenate([x1 * cost - x2 * sint, x2 * cost + x1 * sint, q[:, ROT_DIM:]], axis=1)
    qt_ref[0] = (q * (B_DH ** -0.5 * LOG2E)).reshape(BQ_WIDTH, tb).astype(BF16)

    vt_ref[0] = proj_t(TOFF_V, TOFF_GATE).astype(BF16)
    gtt_ref[0] = proj_t(TOFF_GATE, TOFF_FQ)

    cmp_ref[0] = proj(OFF_KC, OFF_VC)
    cmp_ref[1] = proj(OFF_VC, OFF_KS)

    cos, sl, sh = cos_ref[...], sl_ref[...], sh_ref[...]
    ones_c = _head_ones(C_WIDTH)
    lane = _iota((tb, BKV_WIDTH), 1)
    tok = (pl.program_id(0) % seq_blocks) * tb + _iota((tb, BKV_WIDTH), 0)
    onehot = jnp.where(lane == B_DH + tok // SEL_BLOCK, 1.0, 0.0)
    keys = _head_rmsnorm(proj(OFF_KS, OFF_GATE), _head_ones(2 * BKV_WIDTH))
    for slot in range(2):
        t = _rope(keys[:, slot * BKV_WIDTH:(slot + 1) * BKV_WIDTH] * kg_ref[...], cos, sl, sh)
        for g in range(B_GROUPS):
            low = t if g == 0 else pltpu.roll(t, B_DH, axis=1)
            ka_ref[0, slot * B_GROUPS + g] = jnp.where(lane < B_DH, low, onehot).astype(BF16)

    gt_ref[...] = proj(OFF_GATE, OFF_CK)

    fq = proj_t(TOFF_FQ, TOFF_FV).reshape(C_HEADS, C_DH, tb)
    fq = fq * lax.rsqrt(jnp.mean(fq * fq, axis=1, keepdims=True) + EPS) * fqg_ref[...] * (C_DH ** -0.5 * LOG2E)
    fqt_ref[0] = fq.reshape(C_WIDTH, tb).astype(BF16)
    fvt_ref[0] = proj_t(TOFF_FV, T_ROWS).astype(BF16)
    fk = _head_rmsnorm(proj(OFF_CK, PROJ_WIDTH), ones_c) * fkg_ref[...]
    for hd in range(C_HEADS):
        slab = fk[:, (hd // 2) * BKV_WIDTH:(hd // 2 + 1) * BKV_WIDTH]
        low = slab if hd % 2 == 0 else pltpu.roll(slab, C_DH, axis=1)
        fk_ref[0, hd] = jnp.where(lane < C_DH, low, 0.0).astype(BF16)


def _inproj(x2d, g, w, wt, cos, sl, sh, cost, sint, qg, kg, fqg, fkg, bsz, seq, tb=512):
    n = bsz * seq
    nsb = seq // tb
    row = lambda i: (i, 0)
    const = lambda i: (0, 0)
    pos = lambda i: (i % nsb, 0)
    pos_t = lambda i: (0, i % nsb)
    heads = lambda i: (i // nsb, 0, i % nsb, 0)
    tokens_on_lanes = lambda i: (i // nsb, 0, i % nsb)
    return pl.pallas_call(
        functools.partial(_inproj_body, nsb),
        grid=(n // tb,),
        in_specs=[
            pl.BlockSpec((tb, D_MODEL), row),
            pl.BlockSpec((1, D_MODEL), const),
            pl.BlockSpec((D_MODEL, PROJ_WIDTH), const),
            pl.BlockSpec((T_ROWS, D_MODEL), const),
            pl.BlockSpec((tb, BKV_WIDTH), pos),
            pl.BlockSpec((tb, BKV_WIDTH), pos),
            pl.BlockSpec((tb, BKV_WIDTH), pos),
            pl.BlockSpec((ROT_HALF, tb), pos_t),
            pl.BlockSpec((ROT_HALF, tb), pos_t),
            pl.BlockSpec((B_DH, 1), const),
            pl.BlockSpec((1, BKV_WIDTH), const),
            pl.BlockSpec((C_DH, 1), const),
            pl.BlockSpec((1, C_WIDTH), const),
        ],
        out_specs=[
            pl.BlockSpec((tb, 4 * A_WIDTH), row),
            pl.BlockSpec((1, BQ_WIDTH, tb), tokens_on_lanes),
            pl.BlockSpec((1, 2 * B_GROUPS, tb, BKV_WIDTH), heads),
            pl.BlockSpec((1, 2 * BKV_WIDTH, tb), tokens_on_lanes),
            pl.BlockSpec((2, tb, BKV_WIDTH), lambda i: (0, i, 0)),
            pl.BlockSpec((tb, GATE_LANES), row),
            pl.BlockSpec((1, GATE_ROWS, tb), tokens_on_lanes),
            pl.BlockSpec((1, C_WIDTH, tb), tokens_on_lanes),
            pl.BlockSpec((1, C_HEADS, tb, BKV_WIDTH), heads),
            pl.BlockSpec((1, C_WIDTH, tb), tokens_on_lanes),
        ],
        out_shape=[
            jax.ShapeDtypeStruct((n, 4 * A_WIDTH), F32),
            jax.ShapeDtypeStruct((bsz, BQ_WIDTH, seq), BF16),
            jax.ShapeDtypeStruct((bsz, 2 * B_GROUPS, seq, BKV_WIDTH), BF16),
            jax.ShapeDtypeStruct((bsz, 2 * BKV_WIDTH, seq), BF16),
            jax.ShapeDtypeStruct((2, n, BKV_WIDTH), F32),
            jax.ShapeDtypeStruct((n, GATE_LANES), F32),
            jax.ShapeDtypeStruct((bsz, GATE_ROWS, seq), F32),
            jax.ShapeDtypeStruct((bsz, C_WIDTH, seq), BF16),
            jax.ShapeDtypeStruct((bsz, C_HEADS, seq, BKV_WIDTH), BF16),
            jax.ShapeDtypeStruct((bsz, C_WIDTH, seq), BF16),
        ],
        compiler_params=_params(("parallel",)),
        name="inproj",
    )(x2d, g, w, wt, cos, sl, sh, cost, sint, qg, kg, fqg, fkg)


HGRN_SUB = 16
HGRN_ROWS = 128
HGRN_BATCH = 8


def _hgrn_body(layer, hg_ref, lbl_ref, og_ref, o_ref, st_ref):
    @pl.when(pl.program_id(1) == 0)
    def _():
        st_ref[...] = jnp.zeros_like(st_ref)

    rows, width = HGRN_ROWS, A_WIDTH
    lbl = lbl_ref[...]
    e = jnp.exp(lbl - jnp.max(lbl, axis=0, keepdims=True))
    p = e / jnp.sum(e, axis=0, keepdims=True)
    lb = p[0:1] * 0.0
    for l in range(1, layer + 1):
        lb = lb + p[l:l + 1]

    r = _iota((rows, rows), 0)
    c = _iota((rows, rows), 1)
    tri = jnp.where((r // HGRN_SUB == c // HGRN_SUB) & (c <= r), 1.0, 0.0).astype(BF16)
    ones_h = _head_ones(width)
    tin = _iota((rows, width), 0) % HGRN_SUB
    head_of_lane = _iota((HGRN_SUB, width), 1) // HEAD_DIM

    def per_head(a):
        return jnp.concatenate([jnp.where(head_of_lane == hd, a, 0.0) for hd in range(A_HEADS)], axis=0).astype(BF16)

    for bb in range(hg_ref.shape[0]):
        q = hg_ref[bb, :, 0:width] * (A_DK ** -0.5)
        z = hg_ref[bb, :, width:2 * width]
        v = hg_ref[bb, :, 2 * width:3 * width]
        go = hg_ref[bb, :, 3 * width:4 * width]

        f = lb + (1.0 - lb) * _sigmoid(z)
        logf = jnp.log(jnp.maximum(f, TINY))
        k = (1.0 - lb) * _sigmoid(-z)

        gl = _dot_exact_rhs(tri, logf * LOG2E)

        o_band = jnp.zeros((rows, width), F32)
        for d in range(HGRN_SUB):
            ks, gs, vs = (k, gl, v) if d == 0 else (pltpu.roll(a, d, axis=0) for a in (k, gl, v))
            pr = q * ks * jnp.exp2(jnp.where(tin >= d, gl - gs, NEG_BIG))
            o_band = o_band + _dot(pr.astype(BF16), ones_h) * vs

        st = st_ref[bb]
        outs = []
        for i in range(rows // HGRN_SUB):
            sl_ = slice(i * HGRN_SUB, (i + 1) * HGRN_SUB)
            gi, qi, ki, vi = gl[sl_], q[sl_], k[sl_], v[sl_]
            outs.append(_dot_nt((qi * jnp.exp2(gi)).astype(BF16), st.astype(BF16)))
            glast = gi[HGRN_SUB - 1:HGRN_SUB]
            kdec = ki * jnp.exp2(glast - gi)
            st = st * jnp.exp2(glast) + _dot_tn(per_head(vi), per_head(kdec))
        st_ref[bb] = st
        o = jnp.concatenate(outs, axis=0) + o_band

        o = _head_rmsnorm(o, ones_h) * og_ref[...]
        gate = go * _sigmoid(go)
        o_ref[bb] = (o * gate).astype(o_ref.dtype)


def _hgrn(hg3, lb_logits, og, layer):
    bsz, seq, _ = hg3.shape
    nb = HGRN_BATCH if bsz % HGRN_BATCH == 0 else 1
    return pl.pallas_call(
        functools.partial(_hgrn_body, layer),
        grid=(bsz // nb, seq // HGRN_ROWS),
        in_specs=[
            pl.BlockSpec((nb, HGRN_ROWS, 4 * A_WIDTH), lambda b, i: (b, i, 0)),
            pl.BlockSpec((DEPTH, A_WIDTH), lambda b, i: (0, 0)),
            pl.BlockSpec((1, A_WIDTH), lambda b, i: (0, 0)),
        ],
        out_specs=pl.BlockSpec((nb, HGRN_ROWS, A_WIDTH), lambda b, i: (b, i, 0)),
        out_shape=jax.ShapeDtypeStruct((bsz, seq, A_WIDTH), BF16),
        scratch_shapes=[pltpu.VMEM((nb, A_WIDTH, A_WIDTH), F32)],
        compiler_params=_params(("parallel", "arbitrary")),
        name="hgrn2",
    )(hg3, lb_logits, og)


FOXC_ROWS = 512


FOX_AUG = 16


def _foxc_body(sc_ref, gt_ref, fb_ref, ck_ref, cq_ref, carry_ref):
    @pl.when(pl.program_id(1) == 0)
    def _():
        carry_ref[...] = jnp.zeros_like(carry_ref)

    rows = FOXC_ROWS
    y = gt_ref[0] + fb_ref[...]
    lf = jnp.minimum(y, 0.0) - jnp.log(1.0 + jnp.exp(-jnp.abs(y)))
    tri = jnp.where(_iota((rows, rows), 1) <= _iota((rows, rows), 0), 1.0, 0.0).astype(BF16)
    cc = _dot_exact_rhs(tri, lf) + carry_ref[...]
    carry_ref[...] = cc[rows - 1:rows]
    parts = jnp.concatenate(_split3(cc * LOG2E), axis=1)

    pr = _iota((3 * GATE_LANES, BKV_WIDTH), 0)
    pc = _iota((3 * GATE_LANES, BKV_WIDTH), 1)
    sr = _iota((FOX_AUG, 3 * GATE_LANES), 0)
    sc = _iota((FOX_AUG, 3 * GATE_LANES), 1)
    k_lane = _iota((rows, BKV_WIDTH), 1)
    ones_k = jnp.where(((k_lane >= C_DH) & (k_lane < C_DH + 3)) | (k_lane == C_DH + 6), 1.0, 0.0)
    q_row = _iota((FOX_AUG, rows), 0)
    ones_q = jnp.where((q_row >= 3) & (q_row < 6), 1.0, 0.0)
    ones_q = jnp.where(q_row == 6, -sc_ref[0], ones_q)
    for hd in range(C_HEADS):
        src = FOX_GATE_LANE0 + hd
        place_k = jnp.where((pr % GATE_LANES == src) & (pc == C_DH + 3 + pr // GATE_LANES), -1.0, 0.0).astype(BF16)
        place_q = jnp.where((sc % GATE_LANES == src) & (sr == sc // GATE_LANES), 1.0, 0.0).astype(BF16)
        ck_ref[0, hd] = (_dot(parts, place_k) + ones_k).astype(BF16)
        cq_ref[0, hd] = (_dot_nt(place_q, parts) + ones_q).astype(BF16)


def _foxc(bound, gates3, fb_row):
    bsz, seq, _ = gates3.shape
    return pl.pallas_call(
        _foxc_body,
        grid=(bsz, seq // FOXC_ROWS),
        in_specs=[
            pl.BlockSpec(memory_space=pltpu.SMEM),
            pl.BlockSpec((1, FOXC_ROWS, GATE_LANES), lambda b, i: (b, i, 0)),
            pl.BlockSpec((1, GATE_LANES), lambda b, i: (0, 0)),
        ],
        out_specs=[
            pl.BlockSpec((1, C_HEADS, FOXC_ROWS, BKV_WIDTH), lambda b, i: (b, 0, i, 0)),
            pl.BlockSpec((1, C_HEADS, FOX_AUG, FOXC_ROWS), lambda b, i: (b, 0, 0, i)),
        ],
        out_shape=[
            jax.ShapeDtypeStruct((bsz, C_HEADS, seq, BKV_WIDTH), BF16),
            jax.ShapeDtypeStruct((bsz, C_HEADS, FOX_AUG, seq), BF16),
        ],
        scratch_shapes=[pltpu.VMEM((1, GATE_LANES), F32)],
        compiler_params=_params(("parallel", "arbitrary")),
        name="fox_cumgate",
    )(bound, gates3, fb_row)


FOX_BLK = 512


def _fox_body(qi_ref, kj_ref, sc_ref, qt_ref, cq_ref, k_ref, ck_ref, vt_ref, o_ref, m_ref, acc_ref):
    qi, kj = qi_ref[pl.program_id(1)], kj_ref[pl.program_id(1)]
    blk = FOX_BLK
    use_fixed = sc_ref[1] > 0.5

    @pl.when(kj == 0)
    def _():
        m_ref[...] = jnp.full_like(m_ref, NEG_BIG)
        acc_ref[...] = jnp.zeros_like(acc_ref)

    def update(diagonal, fixed):
        pad = jnp.zeros((BKV_WIDTH - C_DH - FOX_AUG, blk), BF16)
        for hd in range(C_HEADS):
            rows = slice(hd * C_DH, (hd + 1) * C_DH)
            q_aug = jnp.concatenate([qt_ref[0, rows, :], cq_ref[0, hd], pad], axis=0)
            s = _dot(k_ref[0, hd] + ck_ref[0, hd], q_aug)
            if diagonal:
                s = jnp.where(_iota((blk, blk), 0) <= _iota((blk, blk), 1), s, NEG_BIG)
            carry = (acc_ref[hd],) if fixed else (m_ref[hd], acc_ref[hd])
            carry = _attn_step(s, _with_sum_rows(vt_ref[0, rows, :]), carry, fixed)
            acc_ref[hd] = carry[-1]
            if not fixed:
                m_ref[hd] = carry[0]
            if diagonal:
                o_ref[0, rows, :] = _attn_finish(carry).astype(o_ref.dtype)

    for fixed in (True, False):
        pick = use_fixed if fixed else jnp.logical_not(use_fixed)
        pl.when(pick & (kj < qi))(functools.partial(update, False, fixed))
        pl.when(pick & (kj == qi))(functools.partial(update, True, fixed))


def _fox(bound, fqt, cq, fk, ck, fvt):
    bsz, _, seq = fqt.shape
    nb = seq // FOX_BLK
    pairs = [(i, j) for i in range(nb) for j in range(i + 1)]
    qi = jnp.asarray([p[0] for p in pairs], jnp.int32)
    kj = jnp.asarray([p[1] for p in pairs], jnp.int32)
    q_lanes = lambda b, p, qi, kj: (b, 0, qi[p])
    k_rows = lambda b, p, qi, kj: (b, 0, kj[p], 0)
    grid_spec = pltpu.PrefetchScalarGridSpec(
        num_scalar_prefetch=2,
        grid=(bsz, len(pairs)),
        in_specs=[
            pl.BlockSpec(memory_space=pltpu.SMEM),
            pl.BlockSpec((1, C_WIDTH, FOX_BLK), q_lanes),
            pl.BlockSpec((1, C_HEADS, FOX_AUG, FOX_BLK), lambda b, p, qi, kj: (b, 0, 0, qi[p])),
            pl.BlockSpec((1, C_HEADS, FOX_BLK, BKV_WIDTH), k_rows),
            pl.BlockSpec((1, C_HEADS, FOX_BLK, BKV_WIDTH), k_rows),
            pl.BlockSpec((1, C_WIDTH, FOX_BLK), lambda b, p, qi, kj: (b, 0, kj[p])),
        ],
        out_specs=pl.BlockSpec((1, C_WIDTH, FOX_BLK), q_lanes),
        scratch_shapes=[
            pltpu.VMEM((C_HEADS, 1, FOX_BLK), F32),
            pltpu.VMEM((C_HEADS, C_DH + SUM_ROWS, FOX_BLK), F32),
        ],
    )
    return pl.pallas_call(
        _fox_body,
        grid_spec=grid_spec,
        out_shape=jax.ShapeDtypeStruct((bsz, C_WIDTH, seq), BF16),
        compiler_params=_params(("parallel", "arbitrary")),
        name="fox_attn",
    )(qi, kj, bound, fqt, cq, fk, ck, fvt)


def _cmp_body(x_ref, pos_ref, w1_ref, w2k_ref, w2vt_ref, kg_ref, cos_ref, sl_ref, sh_ref, kc_ref, vct_ref):
    n_blk = x_ref.shape[2] // CMP_STRIDE
    hw = B_GROUPS * CMP_HIDDEN
    first = [jnp.zeros((n_blk, hw), F32) for _ in range(2)]
    second = [jnp.zeros((n_blk, hw), F32) for _ in range(2)]
    for l in range(CMP_STRIDE):
        for which in range(2):
            xw = x_ref[which, 0, pl.ds(l, n_blk, stride=CMP_STRIDE), :]
            a = (xw + pos_ref[which, l:l + 1, :]).astype(BF16)
            b = (xw + pos_ref[which, CMP_STRIDE + l:CMP_STRIDE + l + 1, :]).astype(BF16)
            first[which] = first[which] + _dot(a, w1_ref[which, l])
            second[which] = second[which] + _dot(b, w1_ref[which, CMP_STRIDE + l])
    hids = []
    for which in range(2):
        hid = first[which] + pltpu.roll(second[which], n_blk - 1, axis=0)
        hid = 0.5 * hid * (1.0 + jnp.tanh(math.sqrt(2.0 / math.pi) * (hid + 0.044715 * hid * hid * hid)))
        hids.append(hid.astype(BF16))
    kc = _dot(hids[0], w2k_ref[...])
    kc = _head_rmsnorm(kc, _head_ones(BKV_WIDTH)) * kg_ref[...]
    kc = _rope(kc, cos_ref[...], sl_ref[...], sh_ref[...])
    vct = _dot_nt(w2vt_ref[...], hids[1])
    lane = _iota((n_blk, BKV_WIDTH), 1)
    for g in range(B_GROUPS):
        low = kc if g == 0 else pltpu.roll(kc, B_DH, axis=1)
        kc_ref[0, g] = jnp.where(lane < B_DH, low, jnp.where(lane == B_DH, 1.0, 0.0)).astype(BF16)
        vct_ref[0, g] = vct[g * B_DH:(g + 1) * B_DH].astype(BF16)


def _compress(cmp4, pos_e, w1_e, w2k_e, w2vt_e, kg, cos, sl, sh):
    _, bsz, seq, _ = cmp4.shape
    n_blk = seq // CMP_STRIDE
    c2 = lambda b: (0, 0)
    c3 = lambda b: (0, 0, 0)
    c4 = lambda b: (0, 0, 0, 0)
    return pl.pallas_call(
        _cmp_body,
        grid=(bsz,),
        in_specs=[
            pl.BlockSpec((2, 1, seq, BKV_WIDTH), lambda b: (0, b, 0, 0)),
            pl.BlockSpec((2, CMP_BLOCK, BKV_WIDTH), c3),
            pl.BlockSpec((2, CMP_BLOCK, BKV_WIDTH, B_GROUPS * CMP_HIDDEN), c4),
            pl.BlockSpec((B_GROUPS * CMP_HIDDEN, BKV_WIDTH), c2),
            pl.BlockSpec((BKV_WIDTH, B_GROUPS * CMP_HIDDEN), c2),
            pl.BlockSpec((1, BKV_WIDTH), c2),
            pl.BlockSpec((n_blk, BKV_WIDTH), c2),
            pl.BlockSpec((n_blk, BKV_WIDTH), c2),
            pl.BlockSpec((n_blk, BKV_WIDTH), c2),
        ],
        out_specs=[
            pl.BlockSpec((1, B_GROUPS, n_blk, BKV_WIDTH), lambda b: (b, 0, 0, 0)),
            pl.BlockSpec((1, B_GROUPS, B_DH, n_blk), lambda b: (b, 0, 0, 0)),
        ],
        out_shape=[
            jax.ShapeDtypeStruct((bsz, B_GROUPS, n_blk, BKV_WIDTH), BF16),
            jax.ShapeDtypeStruct((bsz, B_GROUPS, B_DH, n_blk), BF16),
        ],
        compiler_params=_params(("parallel",)),
        name="nsa_compress",
    )(cmp4, pos_e, w1_e, w2k_e, w2vt_e, kg, cos, sl, sh)


NSA_Q = 256
NSA_KB = 256
assert NSA_KB % NSA_Q == 0, "a query block must sit inside one key block (single diagonal step)"
SEL_UNROLL = 4


def _nsa_body(sc_ref, qt_ref, kc_ref, vct_ref, ka_ref, vt_ref, gt_ref, o_ref):
    tq = NSA_Q
    lanes = B_HPG * tq
    n_cmp = kc_ref.shape[2]
    n_sel = ka_ref.shape[2] // SEL_BLOCK
    q0 = pl.program_id(1) * tq
    t_lane = q0 + _iota((1, lanes), 1) % tq
    gates = _sigmoid(gt_ref[0])

    cn = _iota((n_sel, n_cmp), 1)
    cj = _iota((n_sel, n_cmp), 0)
    overlap_t = jnp.where((cn * CMP_STRIDE <= cj * SEL_BLOCK + SEL_BLOCK - 1)
                          & (cn * CMP_STRIDE + CMP_BLOCK - 1 >= cj * SEL_BLOCK), 1.0, 0.0).astype(BF16)
    jt = _iota((n_sel, tq), 0)
    row8 = _iota((8, tq), 0)
    tt = q0 + _iota((n_sel, tq), 1)
    cur = tt // SEL_BLOCK
    cmp_end = _iota((n_cmp, lanes), 0) * CMP_STRIDE + CMP_BLOCK - 1
    key_kb = _iota((NSA_KB, lanes), 0)
    kb_diag = q0 // NSA_KB
    offset = sc_ref[0]
    neg_offset = jnp.full((B_DH, lanes), -offset, F32).astype(BF16)

    head_rows = [[slice((g * B_HPG + hd) * B_DH, (g * B_HPG + hd + 1) * B_DH) for hd in range(B_HPG)]
                 for g in range(B_GROUPS)]
    def prepare(g, fixed):
        qt = jnp.concatenate([qt_ref[0, r, :] for r in head_rows[g]], axis=1)
        q_win = jnp.concatenate([qt, neg_offset], axis=0)

        mask_c = cmp_end <= t_lane
        s_c = jnp.where(mask_c, _dot(kc_ref[0, g], q_win), NEG_BIG)
        if fixed:
            p_c = jnp.exp2(s_c)
        else:
            p_c = jnp.where(mask_c, jnp.exp2(s_c - jnp.max(s_c, axis=0, keepdims=True)), 0.0)
        p_c = p_c * (1.0 / jnp.maximum(jnp.sum(p_c, axis=0, keepdims=True), TINY))
        o_cmp = _dot(vct_ref[0, g], p_c.astype(BF16))

        p_sum = p_c[:, 0:tq]
        for hd in range(1, B_HPG):
            p_sum = p_sum + p_c[:, hd * tq:(hd + 1) * tq]
        p_hi = p_sum.astype(BF16)
        p_lo = (p_sum - p_hi.astype(F32)).astype(BF16)
        imp = _dot(overlap_t, p_hi) + _dot(overlap_t, p_lo)
        imp = jnp.where((jt == 0) | (jt == cur) | (jt == cur - 1), POS_BIG, imp)
        imp = jnp.where(jt * SEL_BLOCK > tt, NEG_BIG, imp)
        tiles = [imp[r:r + 8] for r in range(0, n_sel, 8)]

        def count_group(t2, ranks):
            ranks = list(ranks)
            for j2 in range(8 * t2, 8 * t2 + 8):
                other = imp[j2:j2 + 1, :]
                for ti, tile in enumerate(tiles):
                    if ti > t2:
                        beats = other >= tile
                    elif ti < t2:
                        beats = other > tile
                    else:
                        beats = (other > tile) | ((other == tile) & (j2 % 8 < row8))
                    ranks[ti] = ranks[ti] + jnp.where(beats, 1.0, 0.0)
            return tuple(ranks)

        ranks = tuple(jnp.zeros((8, tq), F32) for _ in tiles)
        for t2 in range(len(tiles)):
            ranks = lax.cond(t2 * 8 * SEL_BLOCK <= q0 + tq - 1, functools.partial(count_group, t2), lambda r: r, ranks)
        rank = jnp.concatenate(ranks, axis=0)
        bias = jnp.where((rank < SEL_TOPK) & (imp > NEG_BIG * 0.5), -offset, NEG_BIG).astype(BF16)
        if n_sel < BKV_WIDTH - B_DH:
            bias = jnp.concatenate([bias, jnp.zeros((BKV_WIDTH - B_DH - n_sel, tq), BF16)], axis=0)
        q_sel = jnp.concatenate([qt, jnp.concatenate([bias] * B_HPG, axis=1)], axis=0)
        return o_cmp, q_sel, q_win

    def scores(slot, g, start, n_keys, q_aug):
        st = pl.multiple_of(start, NSA_KB)
        s = _dot(ka_ref[0, slot * B_GROUPS + g, pl.ds(st, n_keys), :], q_aug)
        vt = vt_ref[0, (slot * B_GROUPS + g) * B_DH:(slot * B_GROUPS + g + 1) * B_DH, pl.ds(st, n_keys)]
        return s, _with_sum_rows(vt)

    def gate_row(branch, g):
        base = branch * B_HEADS + g * B_HPG
        return jnp.concatenate([gates[base + hd:base + hd + 1, :] for hd in range(B_HPG)], axis=1)

    def attend(fixed):
        o_cmp, q_sel, q_win = zip(*(prepare(g, fixed) for g in range(B_GROUPS)))
        init = (_attn_init(lanes, fixed),) * B_GROUPS

        def sel_blocks(kb, carry, n_blocks):
            carry = list(carry)
            for u in range(n_blocks):
                for g in range(B_GROUPS):
                    s, vt = scores(0, g, (kb + u) * NSA_KB, NSA_KB, q_sel[g])
                    carry[g] = _attn_step(s, vt, carry[g], fixed)
            return tuple(carry)

        carry = lax.fori_loop(0, kb_diag // SEL_UNROLL, lambda i, c: sel_blocks(SEL_UNROLL * i, c, SEL_UNROLL), init)
        carry = lax.fori_loop(kb_diag - kb_diag % SEL_UNROLL, kb_diag, lambda kb, c: sel_blocks(kb, c, 1), carry)
        o_sel = []
        for g in range(B_GROUPS):
            s, vt = scores(0, g, q0, NSA_KB, q_sel[g])
            s = jnp.where(q0 + key_kb <= t_lane, s, NEG_BIG)
            o_sel.append(_attn_finish(_attn_step(s, vt, carry[g], fixed)))

        win_start = jnp.maximum(q0 - WINDOW, 0)
        kpos = win_start + _iota((WINDOW + NSA_Q, lanes), 0)
        mask_w = (kpos <= t_lane) & (kpos > t_lane - WINDOW)
        o_win = []
        for g in range(B_GROUPS):
            s, vt = scores(1, g, win_start, WINDOW + NSA_Q, q_win[g])
            o_win.append(_attn_finish(_attn_step(jnp.where(mask_w, s, NEG_BIG), vt, init[g], fixed)))

        for g in range(B_GROUPS):
            o = gate_row(0, g) * o_cmp[g] + gate_row(1, g) * o_sel[g] + gate_row(2, g) * o_win[g]
            for hd in range(B_HPG):
                o_ref[0, head_rows[g][hd], :] = o[:, hd * tq:(hd + 1) * tq].astype(o_ref.dtype)

    use_fixed = sc_ref[1] > 0.5
    pl.when(use_fixed)(functools.partial(attend, True))
    pl.when(jnp.logical_not(use_fixed))(functools.partial(attend, False))


def _nsa(bound, qt, kc, vct, ka, vt, gates_t):
    bsz, _, seq = qt.shape
    n_cmp = kc.shape[2]
    assert seq // SEL_BLOCK <= BKV_WIDTH - B_DH, "one selection-block id per spare key lane"
    assert seq >= WINDOW + NSA_Q and WINDOW % NSA_KB == 0
    whole = lambda b, i: (b, 0, 0, 0)
    return pl.pallas_call(
        _nsa_body,
        grid=(bsz, seq // NSA_Q),
        in_specs=[
            pl.BlockSpec(memory_space=pltpu.SMEM),
            pl.BlockSpec((1, BQ_WIDTH, NSA_Q), lambda b, i: (b, 0, i)),
            pl.BlockSpec((1, B_GROUPS, n_cmp, BKV_WIDTH), whole),
            pl.BlockSpec((1, B_GROUPS, B_DH, n_cmp), whole),
            pl.BlockSpec((1, 2 * B_GROUPS, seq, BKV_WIDTH), whole),
            pl.BlockSpec((1, 2 * BKV_WIDTH, seq), lambda b, i: (b, 0, 0)),
            pl.BlockSpec((1, GATE_ROWS, NSA_Q), lambda b, i: (b, 0, i)),
        ],
        out_specs=pl.BlockSpec((1, BQ_WIDTH, NSA_Q), lambda b, i: (b, 0, i)),
        out_shape=jax.ShapeDtypeStruct((bsz, BQ_WIDTH, seq), BF16),
        compiler_params=_params(("parallel", "arbitrary")),
        name="nsa_attn",
    )(bound, qt, kc, vct, ka, vt, gates_t)


FFN_ROWS = 1024
FFN_COLS = 1024


def _mix_ffn_body(x_ref, oa_ref, ob_ref, oc_ref, wo_ref, g_ref, wu_ref, wd_ref, o_ref, h_ref, acc_ref):
    j = pl.program_id(1)

    @pl.when(j == 0)
    def _():
        x = x_ref[...] + _dot(oa_ref[...], wo_ref[0:A_WIDTH, :])
        x = x + _dot_tn(ob_ref[0], wo_ref[A_WIDTH:A_WIDTH + BQ_WIDTH, :])
        x = x + _dot_tn(oc_ref[0], wo_ref[A_WIDTH + BQ_WIDTH:, :])
        var = jnp.mean(x * x, axis=-1, keepdims=True)
        h_ref[...] = (x * lax.rsqrt(var + EPS) * g_ref[...]).astype(BF16)
        acc_ref[...] = x

    u = jnp.maximum(_dot(h_ref[...], wu_ref[...]), 0.0)
    acc_ref[...] += _dot((u * u).astype(BF16), wd_ref[...])

    @pl.when(j == pl.num_programs(1) - 1)
    def _():
        o_ref[...] = acc_ref[...]


def _mix_ffn(x2d, oa2d, ob, oc, w_o, g, w_up, w_down, bsz, seq):
    n = bsz * seq
    nsb = seq // FFN_ROWS
    row = lambda i, j: (i, 0)
    const = lambda i, j: (0, 0)
    tokens_on_lanes = lambda i, j: (i // nsb, 0, i % nsb)
    return pl.pallas_call(
        _mix_ffn_body,
        grid=(n // FFN_ROWS, D_FF // FFN_COLS),
        in_specs=[
            pl.BlockSpec((FFN_ROWS, D_MODEL), row),
            pl.BlockSpec((FFN_ROWS, A_WIDTH), row),
            pl.BlockSpec((1, BQ_WIDTH, FFN_ROWS), tokens_on_lanes),
            pl.BlockSpec((1, C_WIDTH, FFN_ROWS), tokens_on_lanes),
            pl.BlockSpec((D_MODEL, D_MODEL), const),
            pl.BlockSpec((1, D_MODEL), const),
            pl.BlockSpec((D_MODEL, FFN_COLS), lambda i, j: (0, j)),
            pl.BlockSpec((FFN_COLS, D_MODEL), lambda i, j: (j, 0)),
        ],
        out_specs=pl.BlockSpec((FFN_ROWS, D_MODEL), row),
        out_shape=jax.ShapeDtypeStruct((n, D_MODEL), F32),
        scratch_shapes=[pltpu.VMEM((FFN_ROWS, D_MODEL), BF16), pltpu.VMEM((FFN_ROWS, D_MODEL), F32)],
        compiler_params=_params(("parallel", "arbitrary")),
        name="mix_ffn",
    )(x2d, oa2d, ob, oc, w_o, g, w_up, w_down)


def _rope_tables(pos, n_heads):
    inv_freq = jnp.power(jnp.float32(ROPE_THETA), -jnp.arange(0, ROT_DIM, 2, dtype=F32) / ROT_DIM)
    ang = pos.astype(F32)[:, None] * inv_freq[None, :]
    cos, sin = jnp.cos(ang), jnp.sin(ang)
    n = pos.shape[0]
    pad = jnp.zeros((n, HEAD_DIM - ROT_DIM), F32)
    zero = jnp.zeros((n, ROT_HALF), F32)
    c = jnp.concatenate([cos, cos, pad + 1.0], axis=1)
    s_lo = jnp.concatenate([zero, sin, pad], axis=1)
    s_hi = jnp.concatenate([-sin, zero, pad], axis=1)
    return tuple(jnp.tile(t, (1, n_heads)) for t in (c, s_lo, s_hi))


def _rope_tables_t(pos):
    inv_freq = jnp.power(jnp.float32(ROPE_THETA), -jnp.arange(0, ROT_DIM, 2, dtype=F32) / ROT_DIM)
    ang = pos.astype(F32)[:, None] * inv_freq[None, :]
    return jnp.cos(ang).T, jnp.sin(ang).T


def _pack_w_in(w):
    w = w.astype(BF16)
    cols = lambda lo, width: w[:, lo:lo + width]
    gates = jnp.concatenate([cols(SRC_BG, N_GATES), cols(SRC_CF, C_HEADS)], axis=1)
    pad = lambda width: jnp.zeros((D_MODEL, width - N_GATES - C_HEADS), BF16)
    w_rows = jnp.concatenate([cols(SRC_A, 4 * A_WIDTH), cols(SRC_KC, 2 * BKV_WIDTH), cols(SRC_KS, BKV_WIDTH),
                              cols(SRC_KW, BKV_WIDTH), gates, pad(GATE_LANES), cols(SRC_CK, C_WIDTH)], axis=1)
    w_lanes = jnp.concatenate([cols(SRC_BQ, BQ_WIDTH), cols(SRC_VS, BKV_WIDTH), cols(SRC_VW, BKV_WIDTH),
                               gates, pad(GATE_ROWS), cols(SRC_CQ, C_WIDTH), cols(SRC_CV, C_WIDTH)], axis=1)
    return w_rows, w_lanes.T


def _expand_groups(w):
    z = jnp.zeros_like(w)
    top = jnp.concatenate([w, z], axis=-1)
    bot = jnp.concatenate([z, w], axis=-1)
    return jnp.concatenate([top, bot], axis=-2)


def kernel(x, norm1_g, w_in, hgrn_lb_logits, hgrn_onorm_g, nsa_qn_g, nsa_kn_g, nsa_cmp_pos, nsa_cmp_w1,
           nsa_cmp_w2, fox_qn_g, fox_kn_g, fox_fb, w_o, norm2_g, w_up, w_down):
    bsz, seq, _ = x.shape
    n = bsz * seq
    n_blk = seq // CMP_STRIDE
    cos, s_lo, s_hi = _rope_tables(jnp.arange(seq), B_GROUPS)
    cos_t, sin_t = _rope_tables_t(jnp.arange(seq))
    ccos, cs_lo, cs_hi = _rope_tables(jnp.arange(n_blk) * CMP_STRIDE + CMP_BLOCK - 1, B_GROUPS)

    x2d = x.reshape(n, D_MODEL)
    for layer in range(DEPTH):
        row = lambda v, reps: jnp.tile(v[layer].astype(F32), reps)[None, :]
        w_rows, w_lanes = _pack_w_in(w_in[layer])
        hg, qt, ka, vt, cmp2d, gates, gates_t, fqt, fk, fvt = _inproj(
            x2d, norm1_g[layer][None, :], w_rows, w_lanes, cos, s_lo, s_hi, cos_t, sin_t,
            nsa_qn_g[layer].astype(F32)[:, None], row(nsa_kn_g, B_GROUPS), fox_qn_g[layer].astype(F32)[:, None],
            row(fox_kn_g, C_HEADS), bsz, seq)

        o_a = _hgrn(hg.reshape(bsz, seq, 4 * A_WIDTH), hgrn_lb_logits.astype(F32),
                    row(hgrn_onorm_g, A_HEADS), layer)

        gates3 = gates.reshape(bsz, seq, GATE_LANES)
        fb_row = jnp.zeros((1, GATE_LANES), F32).at[0, FOX_GATE_LANE0:FOX_GATE_LANE0 + C_HEADS].set(
            fox_fb[layer].astype(F32))
        fox_bound = _score_bound(fox_qn_g[layer], fox_kn_g[layer])
        ck, cq = _foxc(fox_bound, gates3, fb_row)
        o_c = _fox(fox_bound, fqt, cq, fk, ck, fvt)

        w1 = nsa_cmp_w1[layer].reshape(2, CMP_BLOCK, B_DH, CMP_HIDDEN)
        w1_e = _expand_groups(w1).astype(BF16)
        w2_e = _expand_groups(nsa_cmp_w2[layer]).astype(BF16)
        pos_e = jnp.tile(nsa_cmp_pos[layer].astype(F32), (1, 1, B_GROUPS))
        kc, vct = _compress(cmp2d.reshape(2, bsz, seq, BKV_WIDTH), pos_e, w1_e, w2_e[0], w2_e[1].T,
                            row(nsa_kn_g, B_GROUPS), ccos, cs_lo, cs_hi)
        o_b = _nsa(_score_bound(nsa_qn_g[layer], nsa_kn_g[layer]), qt, kc, vct, ka, vt, gates_t)

        x2d = _mix_ffn(x2d, o_a.reshape(n, A_WIDTH), o_b, o_c, w_o[layer].astype(BF16), norm2_g[layer][None, :],
                       w_up[layer].astype(BF16), w_down[layer].astype(BF16), bsz, seq)
    return x2d.reshape(bsz, seq, D_MODEL)
```

```python
import functools
import math

import numpy as np
import jax
import jax.numpy as jnp
from jax import lax
from jax.experimental import pallas as pl
from jax.experimental.pallas import tpu as pltpu

F32 = jnp.float32
BF16 = jnp.bfloat16

D_MODEL = 1024
DEPTH = 2
A_HEADS, A_DK, A_DV = 4, 64, 64
B_HEADS, B_GROUPS, B_DH = 8, 2, 64
B_HPG = B_HEADS // B_GROUPS
C_HEADS, C_DH = 4, 64
HEAD_DIM = 64
D_FF = 4 * D_MODEL
ROPE_THETA = 500000.0
ROT_DIM = B_DH // 4
ROT_HALF = ROT_DIM // 2
EPS = 1e-6
NEG_BIG = -1e30
POS_BIG = 1e30
TINY = 1e-30
CMP_BLOCK, CMP_STRIDE, CMP_HIDDEN = 32, 16, 128
SEL_BLOCK, SEL_TOPK, WINDOW = 64, 16, 512

A_WIDTH = A_HEADS * A_DK
BQ_WIDTH = B_HEADS * B_DH
BKV_WIDTH = B_GROUPS * B_DH
C_WIDTH = C_HEADS * C_DH
N_GATES = 3 * B_HEADS
GATE_LANES = 128
FOX_GATE_LANE0 = N_GATES

SRC_A = 0
SRC_BQ = 4 * A_WIDTH
SRC_KC = SRC_BQ + BQ_WIDTH
SRC_VC = SRC_KC + BKV_WIDTH
SRC_KS = SRC_VC + BKV_WIDTH
SRC_VS = SRC_KS + BKV_WIDTH
SRC_KW = SRC_VS + BKV_WIDTH
SRC_VW = SRC_KW + BKV_WIDTH
SRC_BG = SRC_VW + BKV_WIDTH
SRC_CQ = SRC_BG + N_GATES
SRC_CK = SRC_CQ + C_WIDTH
SRC_CV = SRC_CK + C_WIDTH
SRC_CF = SRC_CV + C_WIDTH

OFF_A = 0
OFF_KC = 4 * A_WIDTH
OFF_VC = OFF_KC + BKV_WIDTH
OFF_KS = OFF_VC + BKV_WIDTH
OFF_KW = OFF_KS + BKV_WIDTH
OFF_GATE = OFF_KW + BKV_WIDTH
OFF_CK = OFF_GATE + GATE_LANES
PROJ_WIDTH = OFF_CK + C_WIDTH

VMEM_LIMIT = 56 * 1024 * 1024


def _params(sem):
    return pltpu.CompilerParams(dimension_semantics=sem, vmem_limit_bytes=VMEM_LIMIT)


def _dot(a, b):
    return jnp.dot(a, b, preferred_element_type=F32)


def _dot_nt(a, b):
    return lax.dot_general(a, b, (((1,), (1,)), ((), ())), preferred_element_type=F32)


def _dot_tn(a, b):
    return lax.dot_general(a, b, (((0,), (0,)), ((), ())), preferred_element_type=F32)


def _split3(x):
    a = x.astype(BF16)
    r = x - a.astype(F32)
    b = r.astype(BF16)
    c = (r - b.astype(F32)).astype(BF16)
    return a, b, c


def _dot_exact_rhs(m01, x):
    a, b, c = _split3(x)
    return _dot(m01, a) + _dot(m01, b) + _dot(m01, c)


def _iota(shape, dim):
    return lax.broadcasted_iota(jnp.int32, shape, dim)


def _head_ones(width):
    r = _iota((width, width), 0) // HEAD_DIM
    c = _iota((width, width), 1) // HEAD_DIM
    return jnp.where(r == c, 1.0, 0.0).astype(BF16)


def _head_rmsnorm(x, head_ones):
    x2 = x * x
    hi = x2.astype(BF16)
    lo = (x2 - hi.astype(F32)).astype(BF16)
    ms = (_dot(hi, head_ones) + _dot(lo, head_ones)) * (1.0 / HEAD_DIM)
    return x * lax.rsqrt(ms + EPS)


def _rope(x, cos, sin_lo, sin_hi):
    n = x.shape[-1]
    return (x * cos + pltpu.roll(x, ROT_HALF, axis=1) * sin_lo
            + pltpu.roll(x, n - ROT_HALF, axis=1) * sin_hi)


def _sigmoid(x):
    return 1.0 / (1.0 + jnp.exp(-x))


LOG2E = 1.4426950408889634
SUM_ROWS = 16
FIXED_OFFSET_LIMIT = 40.0


def _score_bound(gain_q, gain_k):
    bound = (1.02 * LOG2E * HEAD_DIM ** 0.5) * jnp.max(jnp.abs(gain_q)) * jnp.max(jnp.abs(gain_k))
    fixed = bound < FIXED_OFFSET_LIMIT
    return jnp.stack([jnp.where(fixed, bound, 0.0), jnp.where(fixed, 1.0, 0.0)]).astype(F32)


def _with_sum_rows(vt):
    return jnp.concatenate([vt, jnp.ones((SUM_ROWS, vt.shape[1]), vt.dtype)], axis=0)


def _attn_init(lanes, fixed):
    acc = jnp.zeros((HEAD_DIM + SUM_ROWS, lanes), F32)
    return (acc,) if fixed else (jnp.full((1, lanes), NEG_BIG, F32), acc)


def _attn_step(s, vt_aug, carry, fixed):
    if fixed:
        return (carry[0] + _dot(vt_aug, jnp.exp2(s).astype(BF16)),)
    m_old, acc = carry
    m_new = jnp.maximum(m_old, jnp.max(s, axis=0, keepdims=True))
    p = jnp.exp2(s - m_new)
    return m_new, jnp.exp2(m_old - m_new) * acc + _dot(vt_aug, p.astype(BF16))


def _attn_finish(carry):
    acc = carry[-1]
    return acc[:HEAD_DIM] / jnp.maximum(acc[HEAD_DIM:HEAD_DIM + 1], TINY)


TOFF_Q = 0
TOFF_V = BQ_WIDTH
TOFF_GATE = TOFF_V + 2 * BKV_WIDTH
GATE_ROWS = 32
TOFF_FQ = TOFF_GATE + GATE_ROWS
TOFF_FV = TOFF_FQ + C_WIDTH
T_ROWS = TOFF_FV + C_WIDTH


def _inproj_body(seq_blocks, x_ref, g_ref, w_ref, wt_ref, cos_ref, sl_ref, sh_ref, cost_ref, sint_ref, qg_ref,
                 kg_ref, fqg_ref, fkg_ref, hg_ref, qt_ref, ka_ref, vt_ref, cmp_ref, gt_ref, gtt_ref,
                 fqt_ref, fk_ref, fvt_ref):
    tb = x_ref.shape[0]
    x = x_ref[...]
    var = jnp.mean(x * x, axis=-1, keepdims=True)
    h = (x * lax.rsqrt(var + EPS) * g_ref[...]).astype(BF16)

    rows_proj = _dot(h, w_ref[...])

    def proj(lo, hi):
        return rows_proj[:, lo:hi]

    def proj_t(lo, hi):
        return _dot_nt(wt_ref[lo:hi, :], h)

    hg_ref[...] = proj(OFF_A, OFF_KC)

    q = proj_t(TOFF_Q, TOFF_V).reshape(B_HEADS, B_DH, tb)
    q = q * lax.rsqrt(jnp.mean(q * q, axis=1, keepdims=True) + EPS) * qg_ref[...]
    cost, sint = cost_ref[...], sint_ref[...]
    x1, x2 = q[:, 0:ROT_HALF], q[:, ROT_HALF:ROT_DIM]
    q = jnp.concatenate([x1 * cost - x2 * sint, x2 * cost + x1 * sint, q[:, ROT_DIM:]], axis=1)
    qt_ref[0] = (q * (B_DH ** -0.5 * LOG2E)).reshape(BQ_WIDTH, tb).astype(BF16)

    vt_ref[0] = proj_t(TOFF_V, TOFF_GATE).astype(BF16)
    gtt_ref[0] = proj_t(TOFF_GATE, TOFF_FQ)

    cmp_ref[0] = proj(OFF_KC, OFF_VC)
    cmp_ref[1] = proj(OFF_VC, OFF_KS)

    cos, sl, sh = cos_ref[...], sl_ref[...], sh_ref[...]
    ones_c = _head_ones(C_WIDTH)
    lane = _iota((tb, BKV_WIDTH), 1)
    tok = (pl.program_id(0) % seq_blocks) * tb + _iota((tb, BKV_WIDTH), 0)
    onehot = jnp.where(lane == B_DH + tok // SEL_BLOCK, 1.0, 0.0)
    keys = _head_rmsnorm(proj(OFF_KS, OFF_GATE), _head_ones(2 * BKV_WIDTH))
    for slot in range(2):
        t = _rope(keys[:, slot * BKV_WIDTH:(slot + 1) * BKV_WIDTH] * kg_ref[...], cos, sl, sh)
        for g in range(B_GROUPS):
            low = t if g == 0 else pltpu.roll(t, B_DH, axis=1)
            ka_ref[0, slot * B_GROUPS + g] = jnp.where(lane < B_DH, low, onehot).astype(BF16)

    gt_ref[...] = proj(OFF_GATE, OFF_CK)

    fq = proj_t(TOFF_FQ, TOFF_FV).reshape(C_HEADS, C_DH, tb)
    fq = fq * lax.rsqrt(jnp.mean(fq * fq, axis=1, keepdims=True) + EPS) * fqg_ref[...] * (C_DH ** -0.5 * LOG2E)
    fqt_ref[0] = fq.reshape(C_WIDTH, tb).astype(BF16)
    fvt_ref[0] = proj_t(TOFF_FV, T_ROWS).astype(BF16)
    fk = _head_rmsnorm(proj(OFF_CK, PROJ_WIDTH), ones_c) * fkg_ref[...]
    for hd in range(C_HEADS):
        slab = fk[:, (hd // 2) * BKV_WIDTH:(hd // 2 + 1) * BKV_WIDTH]
        low = slab if hd % 2 == 0 else pltpu.roll(slab, C_DH, axis=1)
        fk_ref[0, hd] = jnp.where(lane < C_DH, low, 0.0).astype(BF16)


def _inproj(x2d, g, w, wt, cos, sl, sh, cost, sint, qg, kg, fqg, fkg, bsz, seq, tb=512):
    n = bsz * seq
    nsb = seq // tb
    row = lambda i: (i, 0)
    const = lambda i: (0, 0)
    pos = lambda i: (i % nsb, 0)
    pos_t = lambda i: (0, i % nsb)
    heads = lambda i: (i // nsb, 0, i % nsb, 0)
    tokens_on_lanes = lambda i: (i // nsb, 0, i % nsb)
    return pl.pallas_call(
        functools.partial(_inproj_body, nsb),
        grid=(n // tb,),
        in_specs=[
            pl.BlockSpec((tb, D_MODEL), row),
            pl.BlockSpec((1, D_MODEL), const),
            pl.BlockSpec((D_MODEL, PROJ_WIDTH), const),
            pl.BlockSpec((T_ROWS, D_MODEL), const),
            pl.BlockSpec((tb, BKV_WIDTH), pos),
            pl.BlockSpec((tb, BKV_WIDTH), pos),
            pl.BlockSpec((tb, BKV_WIDTH), pos),
            pl.BlockSpec((ROT_HALF, tb), pos_t),
            pl.BlockSpec((ROT_HALF, tb), pos_t),
            pl.BlockSpec((B_DH, 1), const),
            pl.BlockSpec((1, BKV_WIDTH), const),
            pl.BlockSpec((C_DH, 1), const),
            pl.BlockSpec((1, C_WIDTH), const),
        ],
        out_specs=[
            pl.BlockSpec((tb, 4 * A_WIDTH), row),
            pl.BlockSpec((1, BQ_WIDTH, tb), tokens_on_lanes),
            pl.BlockSpec((1, 2 * B_GROUPS, tb, BKV_WIDTH), heads),
            pl.BlockSpec((1, 2 * BKV_WIDTH, tb), tokens_on_lanes),
            pl.BlockSpec((2, tb, BKV_WIDTH), lambda i: (0, i, 0)),
            pl.BlockSpec((tb, GATE_LANES), row),
            pl.BlockSpec((1, GATE_ROWS, tb), tokens_on_lanes),
            pl.BlockSpec((1, C_WIDTH, tb), tokens_on_lanes),
            pl.BlockSpec((1, C_HEADS, tb, BKV_WIDTH), heads),
            pl.BlockSpec((1, C_WIDTH, tb), tokens_on_lanes),
        ],
        out_shape=[
            jax.ShapeDtypeStruct((n, 4 * A_WIDTH), F32),
            jax.ShapeDtypeStruct((bsz, BQ_WIDTH, seq), BF16),
            jax.ShapeDtypeStruct((bsz, 2 * B_GROUPS, seq, BKV_WIDTH), BF16),
            jax.ShapeDtypeStruct((bsz, 2 * BKV_WIDTH, seq), BF16),
            jax.ShapeDtypeStruct((2, n, BKV_WIDTH), F32),
            jax.ShapeDtypeStruct((n, GATE_LANES), F32),
            jax.ShapeDtypeStruct((bsz, GATE_ROWS, seq), F32),
            jax.ShapeDtypeStruct((bsz, C_WIDTH, seq), BF16),
            jax.ShapeDtypeStruct((bsz, C_HEADS, seq, BKV_WIDTH), BF16),
            jax.ShapeDtypeStruct((bsz, C_WIDTH, seq), BF16),
        ],
        compiler_params=_params(("parallel",)),
        name="inproj",
    )(x2d, g, w, wt, cos, sl, sh, cost, sint, qg, kg, fqg, fkg)


HGRN_SUB = 16
HGRN_ROWS = 128
HGRN_BATCH = 8


def _hgrn_body(layer, hg_ref, lbl_ref, og_ref, o_ref, st_ref):
    @pl.when(pl.program_id(1) == 0)
    def _():
        st_ref[...] = jnp.zeros_like(st_ref)

    rows, width = HGRN_ROWS, A_WIDTH
    lbl = lbl_ref[...]
    e = jnp.exp(lbl - jnp.max(lbl, axis=0, keepdims=True))
    p = e / jnp.sum(e, axis=0, keepdims=True)
    lb = p[0:1] * 0.0
    for l in range(1, layer + 1):
        lb = lb + p[l:l + 1]

    r = _iota((rows, rows), 0)
    c = _iota((rows, rows), 1)
    tri = jnp.where((r // HGRN_SUB == c // HGRN_SUB) & (c <= r), 1.0, 0.0).astype(BF16)
    ones_h = _head_ones(width)
    tin = _iota((rows, width), 0) % HGRN_SUB
    head_of_lane = _iota((HGRN_SUB, width), 1) // HEAD_DIM

    def per_head(a):
        return jnp.concatenate([jnp.where(head_of_lane == hd, a, 0.0) for hd in range(A_HEADS)], axis=0).astype(BF16)

    for bb in range(hg_ref.shape[0]):
        q = hg_ref[bb, :, 0:width] * (A_DK ** -0.5)
        z = hg_ref[bb, :, width:2 * width]
        v = hg_ref[bb, :, 2 * width:3 * width]
        go = hg_ref[bb, :, 3 * width:4 * width]

        f = lb + (1.0 - lb) * _sigmoid(z)
        logf = jnp.log(jnp.maximum(f, TINY))
        k = (1.0 - lb) * _sigmoid(-z)

        gl = _dot_exact_rhs(tri, logf * LOG2E)

        o_band = jnp.zeros((rows, width), F32)
        for d in range(HGRN_SUB):
            ks, gs, vs = (k, gl, v) if d == 0 else (pltpu.roll(a, d, axis=0) for a in (k, gl, v))
            pr = q * ks * jnp.exp2(jnp.where(tin >= d, gl - gs, NEG_BIG))
            o_band = o_band + _dot(pr.astype(BF16), ones_h) * vs

        st = st_ref[bb]
        outs = []
        for i in range(rows // HGRN_SUB):
            sl_ = slice(i * HGRN_SUB, (i + 1) * HGRN_SUB)
            gi, qi, ki, vi = gl[sl_], q[sl_], k[sl_], v[sl_]
            outs.append(_dot_nt((qi * jnp.exp2(gi)).astype(BF16), st.astype(BF16)))
            glast = gi[HGRN_SUB - 1:HGRN_SUB]
            kdec = ki * jnp.exp2(glast - gi)
            st = st * jnp.exp2(glast) + _dot_tn(per_head(vi), per_head(kdec))
        st_ref[bb] = st
        o = jnp.concatenate(outs, axis=0) + o_band

        o = _head_rmsnorm(o, ones_h) * og_ref[...]
        gate = go * _sigmoid(go)
        o_ref[bb] = (o * gate).astype(o_ref.dtype)


def _hgrn(hg3, lb_logits, og, layer):
    bsz, seq, _ = hg3.shape
    nb = HGRN_BATCH if bsz % HGRN_BATCH == 0 else 1
    return pl.pallas_call(
        functools.partial(_hgrn_body, layer),
        grid=(bsz // nb, seq // HGRN_ROWS),
        in_specs=[
            pl.BlockSpec((nb, HGRN_ROWS, 4 * A_WIDTH), lambda b, i: (b, i, 0)),
            pl.BlockSpec((DEPTH, A_WIDTH), lambda b, i: (0, 0)),
            pl.BlockSpec((1, A_WIDTH), lambda b, i: (0, 0)),
        ],
        out_specs=pl.BlockSpec((nb, HGRN_ROWS, A_WIDTH), lambda b, i: (b, i, 0)),
        out_shape=jax.ShapeDtypeStruct((bsz, seq, A_WIDTH), BF16),
        scratch_shapes=[pltpu.VMEM((nb, A_WIDTH, A_WIDTH), F32)],
        compiler_params=_params(("parallel", "arbitrary")),
        name="hgrn2",
    )(hg3, lb_logits, og)


FOXC_ROWS = 512


FOX_AUG = 16
C_TERMS = 3


def _foxc_body(sc_ref, gt_ref, fb_ref, ck_ref, cq_ref, carry_ref):
    @pl.when(pl.program_id(1) == 0)
    def _():
        carry_ref[...] = jnp.zeros_like(carry_ref)

    rows = FOXC_ROWS
    y = gt_ref[0] + fb_ref[...]
    lf = jnp.minimum(y, 0.0) - jnp.log(1.0 + jnp.exp(-jnp.abs(y)))
    tri = jnp.where(_iota((rows, rows), 1) <= _iota((rows, rows), 0), 1.0, 0.0).astype(BF16)
    cc = _dot_exact_rhs(tri, lf) + carry_ref[...]
    carry_ref[...] = cc[rows - 1:rows]
    parts = jnp.concatenate(_split3(cc * LOG2E), axis=1)

    pr = _iota((C_TERMS * GATE_LANES, C_HEADS * BKV_WIDTH), 0)
    pc = _iota((C_TERMS * GATE_LANES, C_HEADS * BKV_WIDTH), 1)
    place_k = jnp.where((pr % GATE_LANES == FOX_GATE_LANE0 + pc // BKV_WIDTH)
                        & (pc % BKV_WIDTH == C_DH + C_TERMS + pr // GATE_LANES), -1.0, 0.0).astype(BF16)
    sr = _iota((C_HEADS * FOX_AUG, C_TERMS * GATE_LANES), 0)
    sc = _iota((C_HEADS * FOX_AUG, C_TERMS * GATE_LANES), 1)
    place_q = jnp.where((sc % GATE_LANES == FOX_GATE_LANE0 + sr // FOX_AUG)
                        & (sr % FOX_AUG == sc // GATE_LANES), 1.0, 0.0).astype(BF16)
    ck_all = _dot(parts, place_k)
    cq_all = _dot_nt(place_q, parts)

    k_lane = _iota((rows, BKV_WIDTH), 1)
    ones_k = jnp.where(((k_lane >= C_DH) & (k_lane < C_DH + C_TERMS)) | (k_lane == C_DH + 2 * C_TERMS), 1.0, 0.0)
    q_row = _iota((FOX_AUG, rows), 0)
    ones_q = jnp.where((q_row >= C_TERMS) & (q_row < 2 * C_TERMS), 1.0, 0.0)
    ones_q = jnp.where(q_row == 2 * C_TERMS, -sc_ref[0], ones_q)
    for hd in range(C_HEADS):
        ck_ref[0, hd] = (ck_all[:, hd * BKV_WIDTH:(hd + 1) * BKV_WIDTH] + ones_k).astype(BF16)
        cq_ref[0, hd] = (cq_all[hd * FOX_AUG:(hd + 1) * FOX_AUG] + ones_q).astype(BF16)


def _foxc(bound, gates3, fb_row):
    bsz, seq, _ = gates3.shape
    return pl.pallas_call(
        _foxc_body,
        grid=(bsz, seq // FOXC_ROWS),
        in_specs=[
            pl.BlockSpec(memory_space=pltpu.SMEM),
            pl.BlockSpec((1, FOXC_ROWS, GATE_LANES), lambda b, i: (b, i, 0)),
            pl.BlockSpec((1, GATE_LANES), lambda b, i: (0, 0)),
        ],
        out_specs=[
            pl.BlockSpec((1, C_HEADS, FOXC_ROWS, BKV_WIDTH), lambda b, i: (b, 0, i, 0)),
            pl.BlockSpec((1, C_HEADS, FOX_AUG, FOXC_ROWS), lambda b, i: (b, 0, 0, i)),
        ],
        out_shape=[
            jax.ShapeDtypeStruct((bsz, C_HEADS, seq, BKV_WIDTH), BF16),
            jax.ShapeDtypeStruct((bsz, C_HEADS, FOX_AUG, seq), BF16),
        ],
        scratch_shapes=[pltpu.VMEM((1, GATE_LANES), F32)],
        compiler_params=_params(("parallel", "arbitrary")),
        name="fox_cumgate",
    )(bound, gates3, fb_row)


FOX_BLK = 512


def _fox_body(qi_ref, kj_ref, sc_ref, qt_ref, cq_ref, k_ref, ck_ref, vt_ref, o_ref, m_ref, acc_ref):
    qi, kj = qi_ref[pl.program_id(1)], kj_ref[pl.program_id(1)]
    blk = FOX_BLK
    use_fixed = sc_ref[1] > 0.5

    @pl.when(kj == 0)
    def _():
        m_ref[...] = jnp.full_like(m_ref, NEG_BIG)
        acc_ref[...] = jnp.zeros_like(acc_ref)

    def update(diagonal, fixed):
        pad = jnp.zeros((BKV_WIDTH - C_DH - FOX_AUG, blk), BF16)
        for hd in range(C_HEADS):
            rows = slice(hd * C_DH, (hd + 1) * C_DH)
            q_aug = jnp.concatenate([qt_ref[0, rows, :], cq_ref[0, hd], pad], axis=0)
            s = _dot(k_ref[0, hd] + ck_ref[0, hd], q_aug)
            if diagonal:
                s = jnp.where(_iota((blk, blk), 0) <= _iota((blk, blk), 1), s, NEG_BIG)
            carry = (acc_ref[hd],) if fixed else (m_ref[hd], acc_ref[hd])
            carry = _attn_step(s, _with_sum_rows(vt_ref[0, rows, :]), carry, fixed)
            acc_ref[hd] = carry[-1]
            if not fixed:
                m_ref[hd] = carry[0]
            if diagonal:
                o_ref[0, rows, :] = _attn_finish(carry).astype(o_ref.dtype)

    for fixed in (True, False):
        pick = use_fixed if fixed else jnp.logical_not(use_fixed)
        pl.when(pick & (kj < qi))(functools.partial(update, False, fixed))
        pl.when(pick & (kj == qi))(functools.partial(update, True, fixed))


def _fox(bound, fqt, cq, fk, ck, fvt):
    bsz, _, seq = fqt.shape
    nb = seq // FOX_BLK
    pairs = [(i, j) for i in range(nb) for j in range(i + 1)]
    qi = jnp.asarray([p[0] for p in pairs], jnp.int32)
    kj = jnp.asarray([p[1] for p in pairs], jnp.int32)
    q_lanes = lambda b, p, qi, kj: (b, 0, qi[p])
    k_rows = lambda b, p, qi, kj: (b, 0, kj[p], 0)
    grid_spec = pltpu.PrefetchScalarGridSpec(
        num_scalar_prefetch=2,
        grid=(bsz, len(pairs)),
        in_specs=[
            pl.BlockSpec(memory_space=pltpu.SMEM),
            pl.BlockSpec((1, C_WIDTH, FOX_BLK), q_lanes),
            pl.BlockSpec((1, C_HEADS, FOX_AUG, FOX_BLK), lambda b, p, qi, kj: (b, 0, 0, qi[p])),
            pl.BlockSpec((1, C_HEADS, FOX_BLK, BKV_WIDTH), k_rows),
            pl.BlockSpec((1, C_HEADS, FOX_BLK, BKV_WIDTH), k_rows),
            pl.BlockSpec((1, C_WIDTH, FOX_BLK), lambda b, p, qi, kj: (b, 0, kj[p])),
        ],
        out_specs=pl.BlockSpec((1, C_WIDTH, FOX_BLK), q_lanes),
        scratch_shapes=[
            pltpu.VMEM((C_HEADS, 1, FOX_BLK), F32),
            pltpu.VMEM((C_HEADS, C_DH + SUM_ROWS, FOX_BLK), F32),
        ],
    )
    return pl.pallas_call(
        _fox_body,
        grid_spec=grid_spec,
        out_shape=jax.ShapeDtypeStruct((bsz, C_WIDTH, seq), BF16),
        compiler_params=_params(("parallel", "arbitrary")),
        name="fox_attn",
    )(qi, kj, bound, fqt, cq, fk, ck, fvt)


def _cmp_body(x_ref, pos_ref, w1_ref, w2k_ref, w2vt_ref, kg_ref, cos_ref, sl_ref, sh_ref, kc_ref, vct_ref):
    n_blk = x_ref.shape[2] // CMP_STRIDE
    hw = B_GROUPS * CMP_HIDDEN
    first = [jnp.zeros((n_blk, hw), F32) for _ in range(2)]
    second = [jnp.zeros((n_blk, hw), F32) for _ in range(2)]
    for l in range(CMP_STRIDE):
        for which in range(2):
            xw = x_ref[which, 0, pl.ds(l, n_blk, stride=CMP_STRIDE), :]
            a = (xw + pos_ref[which, l:l + 1, :]).astype(BF16)
            b = (xw + pos_ref[which, CMP_STRIDE + l:CMP_STRIDE + l + 1, :]).astype(BF16)
            first[which] = first[which] + _dot(a, w1_ref[which, l])
            second[which] = second[which] + _dot(b, w1_ref[which, CMP_STRIDE + l])
    hids = []
    for which in range(2):
        hid = first[which] + pltpu.roll(second[which], n_blk - 1, axis=0)
        hid = 0.5 * hid * (1.0 + jnp.tanh(math.sqrt(2.0 / math.pi) * (hid + 0.044715 * hid * hid * hid)))
        hids.append(hid.astype(BF16))
    kc = _dot(hids[0], w2k_ref[...])
    kc = _head_rmsnorm(kc, _head_ones(BKV_WIDTH)) * kg_ref[...]
    kc = _rope(kc, cos_ref[...], sl_ref[...], sh_ref[...])
    vct = _dot_nt(w2vt_ref[...], hids[1])
    lane = _iota((n_blk, BKV_WIDTH), 1)
    for g in range(B_GROUPS):
        low = kc if g == 0 else pltpu.roll(kc, B_DH, axis=1)
        kc_ref[0, g] = jnp.where(lane < B_DH, low, jnp.where(lane == B_DH, 1.0, 0.0)).astype(BF16)
        vct_ref[0, g] = vct[g * B_DH:(g + 1) * B_DH].astype(BF16)


def _compress(cmp4, pos_e, w1_e, w2k_e, w2vt_e, kg, cos, sl, sh):
    _, bsz, seq, _ = cmp4.shape
    n_blk = seq // CMP_STRIDE
    c2 = lambda b: (0, 0)
    c3 = lambda b: (0, 0, 0)
    c4 = lambda b: (0, 0, 0, 0)
    return pl.pallas_call(
        _cmp_body,
        grid=(bsz,),
        in_specs=[
            pl.BlockSpec((2, 1, seq, BKV_WIDTH), lambda b: (0, b, 0, 0)),
            pl.BlockSpec((2, CMP_BLOCK, BKV_WIDTH), c3),
            pl.BlockSpec((2, CMP_BLOCK, BKV_WIDTH, B_GROUPS * CMP_HIDDEN), c4),
            pl.BlockSpec((B_GROUPS * CMP_HIDDEN, BKV_WIDTH), c2),
            pl.BlockSpec((BKV_WIDTH, B_GROUPS * CMP_HIDDEN), c2),
            pl.BlockSpec((1, BKV_WIDTH), c2),
            pl.BlockSpec((n_blk, BKV_WIDTH), c2),
            pl.BlockSpec((n_blk, BKV_WIDTH), c2),
            pl.BlockSpec((n_blk, BKV_WIDTH), c2),
        ],
        out_specs=[
            pl.BlockSpec((1, B_GROUPS, n_blk, BKV_WIDTH), lambda b: (b, 0, 0, 0)),
            pl.BlockSpec((1, B_GROUPS, B_DH, n_blk), lambda b: (b, 0, 0, 0)),
        ],
        out_shape=[
            jax.ShapeDtypeStruct((bsz, B_GROUPS, n_blk, BKV_WIDTH), BF16),
            jax.ShapeDtypeStruct((bsz, B_GROUPS, B_DH, n_blk), BF16),
        ],
        compiler_params=_params(("parallel",)),
        name="nsa_compress",
    )(cmp4, pos_e, w1_e, w2k_e, w2vt_e, kg, cos, sl, sh)


NSA_Q = 256
NSA_KB = 256
assert NSA_KB % NSA_Q == 0, "a query block must sit inside one key block (single diagonal step)"
SEL_UNROLL = 4


def _nsa_body(sc_ref, qt_ref, kc_ref, vct_ref, ka_ref, vt_ref, gt_ref, o_ref):
    tq = NSA_Q
    lanes = B_HPG * tq
    n_cmp = kc_ref.shape[2]
    n_sel = ka_ref.shape[2] // SEL_BLOCK
    q0 = pl.program_id(1) * tq
    t_lane = q0 + _iota((1, lanes), 1) % tq
    gates = _sigmoid(gt_ref[0])

    cn = _iota((n_sel, n_cmp), 1)
    cj = _iota((n_sel, n_cmp), 0)
    overlap_t = jnp.where((cn * CMP_STRIDE <= cj * SEL_BLOCK + SEL_BLOCK - 1)
                          & (cn * CMP_STRIDE + CMP_BLOCK - 1 >= cj * SEL_BLOCK), 1.0, 0.0).astype(BF16)
    jt = _iota((n_sel, tq), 0)
    row8 = _iota((8, tq), 0)
    tt = q0 + _iota((n_sel, tq), 1)
    cur = tt // SEL_BLOCK
    cmp_end = _iota((n_cmp, lanes), 0) * CMP_STRIDE + CMP_BLOCK - 1
    key_kb = _iota((NSA_KB, lanes), 0)
    kb_diag = q0 // NSA_KB
    offset = sc_ref[0]
    neg_offset = jnp.full((B_DH, lanes), -offset, F32).astype(BF16)

    head_rows = [[slice((g * B_HPG + hd) * B_DH, (g * B_HPG + hd + 1) * B_DH) for hd in range(B_HPG)]
                 for g in range(B_GROUPS)]
    def prepare(g, fixed):
        qt = jnp.concatenate([qt_ref[0, r, :] for r in head_rows[g]], axis=1)
        q_win = jnp.concatenate([qt, neg_offset], axis=0)

        mask_c = cmp_end <= t_lane
        s_c = jnp.where(mask_c, _dot(kc_ref[0, g], q_win), NEG_BIG)
        if fixed:
            p_c = jnp.exp2(s_c)
        else:
            p_c = jnp.where(mask_c, jnp.exp2(s_c - jnp.max(s_c, axis=0, keepdims=True)), 0.0)
        p_c = p_c * (1.0 / jnp.maximum(jnp.sum(p_c, axis=0, keepdims=True), TINY))
        o_cmp = _dot(vct_ref[0, g], p_c.astype(BF16))

        p_sum = p_c[:, 0:tq]
        for hd in range(1, B_HPG):
            p_sum = p_sum + p_c[:, hd * tq:(hd + 1) * tq]
        p_hi = p_sum.astype(BF16)
        p_lo = (p_sum - p_hi.astype(F32)).astype(BF16)
        imp = _dot(overlap_t, p_hi) + _dot(overlap_t, p_lo)
        imp = jnp.where((jt == 0) | (jt == cur) | (jt == cur - 1), POS_BIG, imp)
        imp = jnp.where(jt * SEL_BLOCK > tt, NEG_BIG, imp)
        tiles = [imp[r:r + 8] for r in range(0, n_sel, 8)]

        def count_group(t2, ranks):
            ranks = list(ranks)
            for j2 in range(8 * t2, 8 * t2 + 8):
                other = imp[j2:j2 + 1, :]
                for ti, tile in enumerate(tiles):
                    if ti > t2:
                        beats = other >= tile
                    elif ti < t2:
                        beats = other > tile
                    else:
                        beats = (other > tile) | ((other == tile) & (j2 % 8 < row8))
                    ranks[ti] = ranks[ti] + jnp.where(beats, 1.0, 0.0)
            return tuple(ranks)

        ranks = tuple(jnp.zeros((8, tq), F32) for _ in tiles)
        for t2 in range(len(tiles)):
            ranks = lax.cond(t2 * 8 * SEL_BLOCK <= q0 + tq - 1, functools.partial(count_group, t2), lambda r: r, ranks)
        rank = jnp.concatenate(ranks, axis=0)
        bias = jnp.where((rank < SEL_TOPK) & (imp > NEG_BIG * 0.5), -offset, NEG_BIG).astype(BF16)
        if n_sel < BKV_WIDTH - B_DH:
            bias = jnp.concatenate([bias, jnp.zeros((BKV_WIDTH - B_DH - n_sel, tq), BF16)], axis=0)
        q_sel = jnp.concatenate([qt, jnp.concatenate([bias] * B_HPG, axis=1)], axis=0)
        return o_cmp, q_sel, q_win

    def scores(slot, g, start, n_keys, q_aug):
        st = pl.multiple_of(start, NSA_KB)
        s = _dot(ka_ref[0, slot * B_GROUPS + g, pl.ds(st, n_keys), :], q_aug)
        vt = vt_ref[0, (slot * B_GROUPS + g) * B_DH:(slot * B_GROUPS + g + 1) * B_DH, pl.ds(st, n_keys)]
        return s, _with_sum_rows(vt)

    def gate_row(branch, g):
        base = branch * B_HEADS + g * B_HPG
        return jnp.concatenate([gates[base + hd:base + hd + 1, :] for hd in range(B_HPG)], axis=1)

    def attend(fixed):
        o_cmp, q_sel, q_win = zip(*(prepare(g, fixed) for g in range(B_GROUPS)))
        init = (_attn_init(lanes, fixed),) * B_GROUPS

        def sel_blocks(kb, carry, n_blocks):
            carry = list(carry)
            for u in range(n_blocks):
                for g in range(B_GROUPS):
                    s, vt = scores(0, g, (kb + u) * NSA_KB, NSA_KB, q_sel[g])
                    carry[g] = _attn_step(s, vt, carry[g], fixed)
            return tuple(carry)

        carry = lax.fori_loop(0, kb_diag // SEL_UNROLL, lambda i, c: sel_blocks(SEL_UNROLL * i, c, SEL_UNROLL), init)
        carry = lax.fori_loop(kb_diag - kb_diag % SEL_UNROLL, kb_diag, lambda kb, c: sel_blocks(kb, c, 1), carry)
        o_sel = []
        for g in range(B_GROUPS):
            s, vt = scores(0, g, q0, NSA_KB, q_sel[g])
            s = jnp.where(q0 + key_kb <= t_lane, s, NEG_BIG)
            o_sel.append(_attn_finish(_attn_step(s, vt, carry[g], fixed)))

        def window(steady):
            if steady:
                win_start = q0 - WINDOW
                later = _iota((tq, lanes), 0) > _iota((tq, lanes), 1) % tq
            else:
                win_start = jnp.maximum(q0 - WINDOW, 0)
                kpos = win_start + _iota((WINDOW + tq, lanes), 0)
                mask_w = (kpos <= t_lane) & (kpos > t_lane - WINDOW)
            outs = []
            for g in range(B_GROUPS):
                s, vt = scores(1, g, win_start, WINDOW + tq, q_win[g])
                if steady:
                    s = jnp.concatenate([jnp.where(later, s[:tq], NEG_BIG), s[tq:WINDOW],
                                         jnp.where(later, NEG_BIG, s[WINDOW:])], axis=0)
                else:
                    s = jnp.where(mask_w, s, NEG_BIG)
                outs.append(_attn_finish(_attn_step(s, vt, init[g], fixed)))
            return tuple(outs)

        o_win = lax.cond(q0 >= WINDOW, functools.partial(window, True), functools.partial(window, False))

        for g in range(B_GROUPS):
            o = gate_row(0, g) * o_cmp[g] + gate_row(1, g) * o_sel[g] + gate_row(2, g) * o_win[g]
            for hd in range(B_HPG):
                o_ref[0, head_rows[g][hd], :] = o[:, hd * tq:(hd + 1) * tq].astype(o_ref.dtype)

    use_fixed = sc_ref[1] > 0.5
    pl.when(use_fixed)(functools.partial(attend, True))
    pl.when(jnp.logical_not(use_fixed))(functools.partial(attend, False))


def _nsa(bound, qt, kc, vct, ka, vt, gates_t):
    bsz, _, seq = qt.shape
    n_cmp = kc.shape[2]
    assert seq // SEL_BLOCK <= BKV_WIDTH - B_DH, "one selection-block id per spare key lane"
    assert seq >= WINDOW + NSA_Q and WINDOW % NSA_KB == 0
    whole = lambda b, i: (b, 0, 0, 0)
    return pl.pallas_call(
        _nsa_body,
        grid=(bsz, seq // NSA_Q),
        in_specs=[
            pl.BlockSpec(memory_space=pltpu.SMEM),
            pl.BlockSpec((1, BQ_WIDTH, NSA_Q), lambda b, i: (b, 0, i)),
            pl.BlockSpec((1, B_GROUPS, n_cmp, BKV_WIDTH), whole),
            pl.BlockSpec((1, B_GROUPS, B_DH, n_cmp), whole),
            pl.BlockSpec((1, 2 * B_GROUPS, seq, BKV_WIDTH), whole),
            pl.BlockSpec((1, 2 * BKV_WIDTH, seq), lambda b, i: (b, 0, 0)),
            pl.BlockSpec((1, GATE_ROWS, NSA_Q), lambda b, i: (b, 0, i)),
        ],
        out_specs=pl.BlockSpec((1, BQ_WIDTH, NSA_Q), lambda b, i: (b, 0, i)),
        out_shape=jax.ShapeDtypeStruct((bsz, BQ_WIDTH, seq), BF16),
        compiler_params=_params(("parallel", "arbitrary")),
        name="nsa_attn",
    )(bound, qt, kc, vct, ka, vt, gates_t)


FFN_ROWS = 1024
FFN_COLS = 1024


def _mix_ffn_body(x_ref, oa_ref, ob_ref, oc_ref, wo_ref, g_ref, wu_ref, wd_ref, o_ref, h_ref, acc_ref):
    j = pl.program_id(1)

    @pl.when(j == 0)
    def _():
        x = x_ref[...] + _dot(oa_ref[...], wo_ref[0:A_WIDTH, :])
        x = x + _dot_tn(ob_ref[0], wo_ref[A_WIDTH:A_WIDTH + BQ_WIDTH, :])
        x = x + _dot_tn(oc_ref[0], wo_ref[A_WIDTH + BQ_WIDTH:, :])
        var = jnp.mean(x * x, axis=-1, keepdims=True)
        h_ref[...] = (x * lax.rsqrt(var + EPS) * g_ref[...]).astype(BF16)
        acc_ref[...] = x

    u = jnp.maximum(_dot(h_ref[...], wu_ref[...]), 0.0)
    acc_ref[...] += _dot((u * u).astype(BF16), wd_ref[...])

    @pl.when(j == pl.num_programs(1) - 1)
    def _():
        o_ref[...] = acc_ref[...]


def _mix_ffn(x2d, oa2d, ob, oc, w_o, g, w_up, w_down, bsz, seq):
    n = bsz * seq
    nsb = seq // FFN_ROWS
    row = lambda i, j: (i, 0)
    const = lambda i, j: (0, 0)
    tokens_on_lanes = lambda i, j: (i // nsb, 0, i % nsb)
    return pl.pallas_call(
        _mix_ffn_body,
        grid=(n // FFN_ROWS, D_FF // FFN_COLS),
        in_specs=[
            pl.BlockSpec((FFN_ROWS, D_MODEL), row),
            pl.BlockSpec((FFN_ROWS, A_WIDTH), row),
            pl.BlockSpec((1, BQ_WIDTH, FFN_ROWS), tokens_on_lanes),
            pl.BlockSpec((1, C_WIDTH, FFN_ROWS), tokens_on_lanes),
            pl.BlockSpec((D_MODEL, D_MODEL), const),
            pl.BlockSpec((1, D_MODEL), const),
            pl.BlockSpec((D_MODEL, FFN_COLS), lambda i, j: (0, j)),
            pl.BlockSpec((FFN_COLS, D_MODEL), lambda i, j: (j, 0)),
        ],
        out_specs=pl.BlockSpec((FFN_ROWS, D_MODEL), row),
        out_shape=jax.ShapeDtypeStruct((n, D_MODEL), F32),
        scratch_shapes=[pltpu.VMEM((FFN_ROWS, D_MODEL), BF16), pltpu.VMEM((FFN_ROWS, D_MODEL), F32)],
        compiler_params=_params(("parallel", "arbitrary")),
        name="mix_ffn",
    )(x2d, oa2d, ob, oc, w_o, g, w_up, w_down)


def _rope_tables(pos, n_heads):
    inv_freq = jnp.power(jnp.float32(ROPE_THETA), -jnp.arange(0, ROT_DIM, 2, dtype=F32) / ROT_DIM)
    ang = pos.astype(F32)[:, None] * inv_freq[None, :]
    cos, sin = jnp.cos(ang), jnp.sin(ang)
    n = pos.shape[0]
    pad = jnp.zeros((n, HEAD_DIM - ROT_DIM), F32)
    zero = jnp.zeros((n, ROT_HALF), F32)
    c = jnp.concatenate([cos, cos, pad + 1.0], axis=1)
    s_lo = jnp.concatenate([zero, sin, pad], axis=1)
    s_hi = jnp.concatenate([-sin, zero, pad], axis=1)
    return tuple(jnp.tile(t, (1, n_heads)) for t in (c, s_lo, s_hi))


def _rope_tables_t(pos):
    inv_freq = jnp.power(jnp.float32(ROPE_THETA), -jnp.arange(0, ROT_DIM, 2, dtype=F32) / ROT_DIM)
    ang = pos.astype(F32)[:, None] * inv_freq[None, :]
    return jnp.cos(ang).T, jnp.sin(ang).T


def _pack_w_in(w):
    w = w.astype(BF16)
    cols = lambda lo, width: w[:, lo:lo + width]
    gates = jnp.concatenate([cols(SRC_BG, N_GATES), cols(SRC_CF, C_HEADS)], axis=1)
    pad = lambda width: jnp.zeros((D_MODEL, width - N_GATES - C_HEADS), BF16)
    w_rows = jnp.concatenate([cols(SRC_A, 4 * A_WIDTH), cols(SRC_KC, 2 * BKV_WIDTH), cols(SRC_KS, BKV_WIDTH),
                              cols(SRC_KW, BKV_WIDTH), gates, pad(GATE_LANES), cols(SRC_CK, C_WIDTH)], axis=1)
    w_lanes = jnp.concatenate([cols(SRC_BQ, BQ_WIDTH), cols(SRC_VS, BKV_WIDTH), cols(SRC_VW, BKV_WIDTH),
                               gates, pad(GATE_ROWS), cols(SRC_CQ, C_WIDTH), cols(SRC_CV, C_WIDTH)], axis=1)
    return w_rows, w_lanes.T


def _expand_groups(w):
    z = jnp.zeros_like(w)
    top = jnp.concatenate([w, z], axis=-1)
    bot = jnp.concatenate([z, w], axis=-1)
    return jnp.concatenate([top, bot], axis=-2)


def kernel(x, norm1_g, w_in, hgrn_lb_logits, hgrn_onorm_g, nsa_qn_g, nsa_kn_g, nsa_cmp_pos, nsa_cmp_w1,
           nsa_cmp_w2, fox_qn_g, fox_kn_g, fox_fb, w_o, norm2_g, w_up, w_down):
    bsz, seq, _ = x.shape
    n = bsz * seq
    n_blk = seq // CMP_STRIDE
    cos, s_lo, s_hi = _rope_tables(jnp.arange(seq), B_GROUPS)
    cos_t, sin_t = _rope_tables_t(jnp.arange(seq))
    ccos, cs_lo, cs_hi = _rope_tables(jnp.arange(n_blk) * CMP_STRIDE + CMP_BLOCK - 1, B_GROUPS)

    x2d = x.reshape(n, D_MODEL)
    for layer in range(DEPTH):
        row = lambda v, reps: jnp.tile(v[layer].astype(F32), reps)[None, :]
        w_rows, w_lanes = _pack_w_in(w_in[layer])
        hg, qt, ka, vt, cmp2d, gates, gates_t, fqt, fk, fvt = _inproj(
            x2d, norm1_g[layer][None, :], w_rows, w_lanes, cos, s_lo, s_hi, cos_t, sin_t,
            nsa_qn_g[layer].astype(F32)[:, None], row(nsa_kn_g, B_GROUPS), fox_qn_g[layer].astype(F32)[:, None],
            row(fox_kn_g, C_HEADS), bsz, seq)

        o_a = _hgrn(hg.reshape(bsz, seq, 4 * A_WIDTH), hgrn_lb_logits.astype(F32),
                    row(hgrn_onorm_g, A_HEADS), layer)

        gates3 = gates.reshape(bsz, seq, GATE_LANES)
        fb_row = jnp.zeros((1, GATE_LANES), F32).at[0, FOX_GATE_LANE0:FOX_GATE_LANE0 + C_HEADS].set(
            fox_fb[layer].astype(F32))
        fox_bound = _score_bound(fox_qn_g[layer], fox_kn_g[layer])
        ck, cq = _foxc(fox_bound, gates3, fb_row)
        o_c = _fox(fox_bound, fqt, cq, fk, ck, fvt)

        w1 = nsa_cmp_w1[layer].reshape(2, CMP_BLOCK, B_DH, CMP_HIDDEN)
        w1_e = _expand_groups(w1).astype(BF16)
        w2_e = _expand_groups(nsa_cmp_w2[layer]).astype(BF16)
        pos_e = jnp.tile(nsa_cmp_pos[layer].astype(F32), (1, 1, B_GROUPS))
        kc, vct = _compress(cmp2d.reshape(2, bsz, seq, BKV_WIDTH), pos_e, w1_e, w2_e[0], w2_e[1].T,
                            row(nsa_kn_g, B_GROUPS), ccos, cs_lo, cs_hi)
        o_b = _nsa(_score_bound(nsa_qn_g[layer], nsa_kn_g[layer]), qt, kc, vct, ka, vt, gates_t)

        x2d = _mix_ffn(x2d, o_a.reshape(n, A_WIDTH), o_b, o_c, w_o[layer].astype(BF16), norm2_g[layer][None, :],
                       w_up[layer].astype(BF16), w_down[layer].astype(BF16), bsz, seq)
    return x2d.reshape(bsz, seq, D_MODEL)
```

```python
import functools
import math

import numpy as np
import jax
import jax.numpy as jnp
from jax import lax
from jax.experimental import pallas as pl
from jax.experimental.pallas import tpu as pltpu

F32 = jnp.float32
BF16 = jnp.bfloat16

D_MODEL = 1024
DEPTH = 2
A_HEADS, A_DK, A_DV = 4, 64, 64
B_HEADS, B_GROUPS, B_DH = 8, 2, 64
B_HPG = B_HEADS // B_GROUPS
C_HEADS, C_DH = 4, 64
HEAD_DIM = 64
D_FF = 4 * D_MODEL
ROPE_THETA = 500000.0
ROT_DIM = B_DH // 4
ROT_HALF = ROT_DIM // 2
EPS = 1e-6
NEG_BIG = -1e30
POS_BIG = 1e30
TINY = 1e-30
CMP_BLOCK, CMP_STRIDE, CMP_HIDDEN = 32, 16, 128
SEL_BLOCK, SEL_TOPK, WINDOW = 64, 16, 512

A_WIDTH = A_HEADS * A_DK
BQ_WIDTH = B_HEADS * B_DH
BKV_WIDTH = B_GROUPS * B_DH
C_WIDTH = C_HEADS * C_DH
N_GATES = 3 * B_HEADS
GATE_LANES = 128
FOX_GATE_LANE0 = N_GATES

SRC_A = 0
SRC_BQ = 4 * A_WIDTH
SRC_KC = SRC_BQ + BQ_WIDTH
SRC_VC = SRC_KC + BKV_WIDTH
SRC_KS = SRC_VC + BKV_WIDTH
SRC_VS = SRC_KS + BKV_WIDTH
SRC_KW = SRC_VS + BKV_WIDTH
SRC_VW = SRC_KW + BKV_WIDTH
SRC_BG = SRC_VW + BKV_WIDTH
SRC_CQ = SRC_BG + N_GATES
SRC_CK = SRC_CQ + C_WIDTH
SRC_CV = SRC_CK + C_WIDTH
SRC_CF = SRC_CV + C_WIDTH

OFF_A = 0
OFF_KC = 4 * A_WIDTH
OFF_VC = OFF_KC + BKV_WIDTH
OFF_KS = OFF_VC + BKV_WIDTH
OFF_KW = OFF_KS + BKV_WIDTH
OFF_GATE = OFF_KW + BKV_WIDTH
OFF_CK = OFF_GATE + GATE_LANES
PROJ_WIDTH = OFF_CK + C_WIDTH

VMEM_LIMIT = 56 * 1024 * 1024


def _params(sem):
    return pltpu.CompilerParams(dimension_semantics=sem, vmem_limit_bytes=VMEM_LIMIT)


def _dot(a, b):
    return jnp.dot(a, b, preferred_element_type=F32)


def _dot_nt(a, b):
    return lax.dot_general(a, b, (((1,), (1,)), ((), ())), preferred_element_type=F32)


def _dot_tn(a, b):
    return lax.dot_general(a, b, (((0,), (0,)), ((), ())), preferred_element_type=F32)


def _split3(x):
    a = x.astype(BF16)
    r = x - a.astype(F32)
    b = r.astype(BF16)
    c = (r - b.astype(F32)).astype(BF16)
    return a, b, c


def _dot_exact_rhs(m01, x):
    a, b, c = _split3(x)
    return _dot(m01, a) + _dot(m01, b) + _dot(m01, c)


def _iota(shape, dim):
    return lax.broadcasted_iota(jnp.int32, shape, dim)


def _head_ones(width):
    r = _iota((width, width), 0) // HEAD_DIM
    c = _iota((width, width), 1) // HEAD_DIM
    return jnp.where(r == c, 1.0, 0.0).astype(BF16)


def _head_rmsnorm(x, head_ones):
    x2 = x * x
    hi = x2.astype(BF16)
    lo = (x2 - hi.astype(F32)).astype(BF16)
    ms = (_dot(hi, head_ones) + _dot(lo, head_ones)) * (1.0 / HEAD_DIM)
    return x * lax.rsqrt(ms + EPS)


def _rope(x, cos, sin_lo, sin_hi):
    n = x.shape[-1]
    return (x * cos + pltpu.roll(x, ROT_HALF, axis=1) * sin_lo
            + pltpu.roll(x, n - ROT_HALF, axis=1) * sin_hi)


def _sigmoid(x):
    return 1.0 / (1.0 + jnp.exp(-x))


LOG2E = 1.4426950408889634
SUM_ROWS = 16
FIXED_OFFSET_LIMIT = 40.0


def _score_bound(gain_q, gain_k):
    bound = (1.02 * LOG2E * HEAD_DIM ** 0.5) * jnp.max(jnp.abs(gain_q)) * jnp.max(jnp.abs(gain_k))
    fixed = bound < FIXED_OFFSET_LIMIT
    return jnp.stack([jnp.where(fixed, bound, 0.0), jnp.where(fixed, 1.0, 0.0)]).astype(F32)


def _with_sum_rows(vt):
    return jnp.concatenate([vt, jnp.ones((SUM_ROWS, vt.shape[1]), vt.dtype)], axis=0)


def _attn_init(lanes, fixed):
    acc = jnp.zeros((HEAD_DIM + SUM_ROWS, lanes), F32)
    return (acc,) if fixed else (jnp.full((1, lanes), NEG_BIG, F32), acc)


def _attn_step(s, vt_aug, carry, fixed):
    if fixed:
        return (carry[0] + _dot(vt_aug, jnp.exp2(s).astype(BF16)),)
    m_old, acc = carry
    m_new = jnp.maximum(m_old, jnp.max(s, axis=0, keepdims=True))
    p = jnp.exp2(s - m_new)
    return m_new, jnp.exp2(m_old - m_new) * acc + _dot(vt_aug, p.astype(BF16))


def _attn_finish(carry):
    acc = carry[-1]
    return acc[:HEAD_DIM] / jnp.maximum(acc[HEAD_DIM:HEAD_DIM + 1], TINY)


TOFF_Q = 0
TOFF_V = BQ_WIDTH
TOFF_GATE = TOFF_V + 2 * BKV_WIDTH
GATE_ROWS = 32
TOFF_FQ = TOFF_GATE + GATE_ROWS
TOFF_FV = TOFF_FQ + C_WIDTH
T_ROWS = TOFF_FV + C_WIDTH


def _inproj_body(seq_blocks, x_ref, g_ref, w_ref, wt_ref, cos_ref, sl_ref, sh_ref, cost_ref, sint_ref, qg_ref,
                 kg_ref, fqg_ref, fkg_ref, hg_ref, qt_ref, ka_ref, vt_ref, cmp_ref, gt_ref, gtt_ref,
                 fqt_ref, fk_ref, fvt_ref):
    tb = x_ref.shape[0]
    x = x_ref[...]
    var = jnp.mean(x * x, axis=-1, keepdims=True)
    h = (x * lax.rsqrt(var + EPS) * g_ref[...]).astype(BF16)

    rows_proj = _dot(h, w_ref[...])

    def proj(lo, hi):
        return rows_proj[:, lo:hi]

    def proj_t(lo, hi):
        return _dot_nt(wt_ref[lo:hi, :], h)

    hg_ref[...] = proj(OFF_A, OFF_KC)

    q = proj_t(TOFF_Q, TOFF_V).reshape(B_HEADS, B_DH, tb)
    q = q * lax.rsqrt(jnp.mean(q * q, axis=1, keepdims=True) + EPS) * qg_ref[...]
    cost, sint = cost_ref[...], sint_ref[...]
    x1, x2 = q[:, 0:ROT_HALF], q[:, ROT_HALF:ROT_DIM]
    q = jnp.concatenate([x1 * cost - x2 * sint, x2 * cost + x1 * sint, q[:, ROT_DIM:]], axis=1)
    qt_ref[0] = (q * (B_DH ** -0.5 * LOG2E)).reshape(BQ_WIDTH, tb).astype(BF16)

    vt_ref[0] = proj_t(TOFF_V, TOFF_GATE).astype(BF16)
    gtt_ref[0] = proj_t(TOFF_GATE, TOFF_FQ)

    cmp_ref[0] = proj(OFF_KC, OFF_VC)
    cmp_ref[1] = proj(OFF_VC, OFF_KS)

    cos, sl, sh = cos_ref[...], sl_ref[...], sh_ref[...]
    ones_c = _head_ones(C_WIDTH)
    lane = _iota((tb, BKV_WIDTH), 1)
    tok = (pl.program_id(0) % seq_blocks) * tb + _iota((tb, BKV_WIDTH), 0)
    onehot = jnp.where(lane == B_DH + tok // SEL_BLOCK, 1.0, 0.0)
    keys = _head_rmsnorm(proj(OFF_KS, OFF_GATE), _head_ones(2 * BKV_WIDTH))
    for slot in range(2):
        t = _rope(keys[:, slot * BKV_WIDTH:(slot + 1) * BKV_WIDTH] * kg_ref[...], cos, sl, sh)
        for g in range(B_GROUPS):
            low = t if g == 0 else pltpu.roll(t, B_DH, axis=1)
            ka_ref[0, slot * B_GROUPS + g] = jnp.where(lane < B_DH, low, onehot).astype(BF16)

    gt_ref[...] = proj(OFF_GATE, OFF_CK)

    fq = proj_t(TOFF_FQ, TOFF_FV).reshape(C_HEADS, C_DH, tb)
    fq = fq * lax.rsqrt(jnp.mean(fq * fq, axis=1, keepdims=True) + EPS) * fqg_ref[...] * (C_DH ** -0.5 * LOG2E)
    fqt_ref[0] = fq.reshape(C_WIDTH, tb).astype(BF16)
    fvt_ref[0] = proj_t(TOFF_FV, T_ROWS).astype(BF16)
    fk = _head_rmsnorm(proj(OFF_CK, PROJ_WIDTH), ones_c) * fkg_ref[...]
    for hd in range(C_HEADS):
        slab = fk[:, (hd // 2) * BKV_WIDTH:(hd // 2 + 1) * BKV_WIDTH]
        low = slab if hd % 2 == 0 else pltpu.roll(slab, C_DH, axis=1)
        fk_ref[0, hd] = jnp.where(lane < C_DH, low, 0.0).astype(BF16)


def _inproj(x2d, g, w, wt, cos, sl, sh, cost, sint, qg, kg, fqg, fkg, bsz, seq, tb=512):
    n = bsz * seq
    nsb = seq // tb
    row = lambda i: (i, 0)
    const = lambda i: (0, 0)
    pos = lambda i: (i % nsb, 0)
    pos_t = lambda i: (0, i % nsb)
    heads = lambda i: (i // nsb, 0, i % nsb, 0)
    tokens_on_lanes = lambda i: (i // nsb, 0, i % nsb)
    return pl.pallas_call(
        functools.partial(_inproj_body, nsb),
        grid=(n // tb,),
        in_specs=[
            pl.BlockSpec((tb, D_MODEL), row),
            pl.BlockSpec((1, D_MODEL), const),
            pl.BlockSpec((D_MODEL, PROJ_WIDTH), const),
            pl.BlockSpec((T_ROWS, D_MODEL), const),
            pl.BlockSpec((tb, BKV_WIDTH), pos),
            pl.BlockSpec((tb, BKV_WIDTH), pos),
            pl.BlockSpec((tb, BKV_WIDTH), pos),
            pl.BlockSpec((ROT_HALF, tb), pos_t),
            pl.BlockSpec((ROT_HALF, tb), pos_t),
            pl.BlockSpec((B_DH, 1), const),
            pl.BlockSpec((1, BKV_WIDTH), const),
            pl.BlockSpec((C_DH, 1), const),
            pl.BlockSpec((1, C_WIDTH), const),
        ],
        out_specs=[
            pl.BlockSpec((tb, 4 * A_WIDTH), row),
            pl.BlockSpec((1, BQ_WIDTH, tb), tokens_on_lanes),
            pl.BlockSpec((1, 2 * B_GROUPS, tb, BKV_WIDTH), heads),
            pl.BlockSpec((1, 2 * BKV_WIDTH, tb), tokens_on_lanes),
            pl.BlockSpec((2, tb, BKV_WIDTH), lambda i: (0, i, 0)),
            pl.BlockSpec((tb, GATE_LANES), row),
            pl.BlockSpec((1, GATE_ROWS, tb), tokens_on_lanes),
            pl.BlockSpec((1, C_WIDTH, tb), tokens_on_lanes),
            pl.BlockSpec((1, C_HEADS, tb, BKV_WIDTH), heads),
            pl.BlockSpec((1, C_WIDTH, tb), tokens_on_lanes),
        ],
        out_shape=[
            jax.ShapeDtypeStruct((n, 4 * A_WIDTH), F32),
            jax.ShapeDtypeStruct((bsz, BQ_WIDTH, seq), BF16),
            jax.ShapeDtypeStruct((bsz, 2 * B_GROUPS, seq, BKV_WIDTH), BF16),
            jax.ShapeDtypeStruct((bsz, 2 * BKV_WIDTH, seq), BF16),
            jax.ShapeDtypeStruct((2, n, BKV_WIDTH), F32),
            jax.ShapeDtypeStruct((n, GATE_LANES), F32),
            jax.ShapeDtypeStruct((bsz, GATE_ROWS, seq), F32),
            jax.ShapeDtypeStruct((bsz, C_WIDTH, seq), BF16),
            jax.ShapeDtypeStruct((bsz, C_HEADS, seq, BKV_WIDTH), BF16),
            jax.ShapeDtypeStruct((bsz, C_WIDTH, seq), BF16),
        ],
        compiler_params=_params(("parallel",)),
        name="inproj",
    )(x2d, g, w, wt, cos, sl, sh, cost, sint, qg, kg, fqg, fkg)


HGRN_SUB = 16
HGRN_ROWS = 128
HGRN_BATCH = 8


def _hgrn_body(layer, hg_ref, lbl_ref, og_ref, o_ref, st_ref):
    @pl.when(pl.program_id(1) == 0)
    def _():
        st_ref[...] = jnp.zeros_like(st_ref)

    rows, width = HGRN_ROWS, A_WIDTH
    lbl = lbl_ref[...]
    e = jnp.exp(lbl - jnp.max(lbl, axis=0, keepdims=True))
    p = e / jnp.sum(e, axis=0, keepdims=True)
    lb = p[0:1] * 0.0
    for l in range(1, layer + 1):
        lb = lb + p[l:l + 1]

    r = _iota((rows, rows), 0)
    c = _iota((rows, rows), 1)
    tri = jnp.where((r // HGRN_SUB == c // HGRN_SUB) & (c <= r), 1.0, 0.0).astype(BF16)
    ones_h = _head_ones(width)
    tin = _iota((rows, width), 0) % HGRN_SUB
    head_of_lane = _iota((HGRN_SUB, width), 1) // HEAD_DIM

    def per_head(a):
        return jnp.concatenate([jnp.where(head_of_lane == hd, a, 0.0) for hd in range(A_HEADS)], axis=0).astype(BF16)

    for bb in range(hg_ref.shape[0]):
        q = hg_ref[bb, :, 0:width] * (A_DK ** -0.5)
        z = hg_ref[bb, :, width:2 * width]
        v = hg_ref[bb, :, 2 * width:3 * width]
        go = hg_ref[bb, :, 3 * width:4 * width]

        f = lb + (1.0 - lb) * _sigmoid(z)
        logf = jnp.log(jnp.maximum(f, TINY))
        k = (1.0 - lb) * _sigmoid(-z)

        gl = _dot_exact_rhs(tri, logf * LOG2E)

        o_band = jnp.zeros((rows, width), F32)
        for d in range(HGRN_SUB):
            ks, gs, vs = (k, gl, v) if d == 0 else (pltpu.roll(a, d, axis=0) for a in (k, gl, v))
            pr = q * ks * jnp.exp2(jnp.where(tin >= d, gl - gs, NEG_BIG))
            o_band = o_band + _dot(pr.astype(BF16), ones_h) * vs

        st = st_ref[bb]
        outs = []
        for i in range(rows // HGRN_SUB):
            sl_ = slice(i * HGRN_SUB, (i + 1) * HGRN_SUB)
            gi, qi, ki, vi = gl[sl_], q[sl_], k[sl_], v[sl_]
            outs.append(_dot_nt((qi * jnp.exp2(gi)).astype(BF16), st.astype(BF16)))
            glast = gi[HGRN_SUB - 1:HGRN_SUB]
            kdec = ki * jnp.exp2(glast - gi)
            st = st * jnp.exp2(glast) + _dot_tn(per_head(vi), per_head(kdec))
        st_ref[bb] = st
        o = jnp.concatenate(outs, axis=0) + o_band

        o = _head_rmsnorm(o, ones_h) * og_ref[...]
        gate = go * _sigmoid(go)
        o_ref[bb] = (o * gate).astype(o_ref.dtype)


def _hgrn(hg3, lb_logits, og, layer):
    bsz, seq, _ = hg3.shape
    nb = HGRN_BATCH if bsz % HGRN_BATCH == 0 else 1
    return pl.pallas_call(
        functools.partial(_hgrn_body, layer),
        grid=(bsz // nb, seq // HGRN_ROWS),
        in_specs=[
            pl.BlockSpec((nb, HGRN_ROWS, 4 * A_WIDTH), lambda b, i: (b, i, 0)),
            pl.BlockSpec((DEPTH, A_WIDTH), lambda b, i: (0, 0)),
            pl.BlockSpec((1, A_WIDTH), lambda b, i: (0, 0)),
        ],
        out_specs=pl.BlockSpec((nb, HGRN_ROWS, A_WIDTH), lambda b, i: (b, i, 0)),
        out_shape=jax.ShapeDtypeStruct((bsz, seq, A_WIDTH), BF16),
        scratch_shapes=[pltpu.VMEM((nb, A_WIDTH, A_WIDTH), F32)],
        compiler_params=_params(("parallel", "arbitrary")),
        name="hgrn2",
    )(hg3, lb_logits, og)


FOXC_ROWS = 512


FOX_AUG = 16
C_TERMS = 3


def _foxc_body(sc_ref, gt_ref, fb_ref, ck_ref, cq_ref, carry_ref):
    @pl.when(pl.program_id(1) == 0)
    def _():
        carry_ref[...] = jnp.zeros_like(carry_ref)

    rows = FOXC_ROWS
    y = gt_ref[0] + fb_ref[...]
    lf = jnp.minimum(y, 0.0) - jnp.log(1.0 + jnp.exp(-jnp.abs(y)))
    tri = jnp.where(_iota((rows, rows), 1) <= _iota((rows, rows), 0), 1.0, 0.0).astype(BF16)
    cc = _dot_exact_rhs(tri, lf) + carry_ref[...]
    carry_ref[...] = cc[rows - 1:rows]
    parts = jnp.concatenate(_split3(cc * LOG2E), axis=1)

    pr = _iota((C_TERMS * GATE_LANES, C_HEADS * BKV_WIDTH), 0)
    pc = _iota((C_TERMS * GATE_LANES, C_HEADS * BKV_WIDTH), 1)
    place_k = jnp.where((pr % GATE_LANES == FOX_GATE_LANE0 + pc // BKV_WIDTH)
                        & (pc % BKV_WIDTH == C_DH + C_TERMS + pr // GATE_LANES), -1.0, 0.0).astype(BF16)
    sr = _iota((C_HEADS * FOX_AUG, C_TERMS * GATE_LANES), 0)
    sc = _iota((C_HEADS * FOX_AUG, C_TERMS * GATE_LANES), 1)
    place_q = jnp.where((sc % GATE_LANES == FOX_GATE_LANE0 + sr // FOX_AUG)
                        & (sr % FOX_AUG == sc // GATE_LANES), 1.0, 0.0).astype(BF16)
    ck_all = _dot(parts, place_k)
    cq_all = _dot_nt(place_q, parts)

    k_lane = _iota((rows, BKV_WIDTH), 1)
    ones_k = jnp.where(((k_lane >= C_DH) & (k_lane < C_DH + C_TERMS)) | (k_lane == C_DH + 2 * C_TERMS), 1.0, 0.0)
    q_row = _iota((FOX_AUG, rows), 0)
    ones_q = jnp.where((q_row >= C_TERMS) & (q_row < 2 * C_TERMS), 1.0, 0.0)
    ones_q = jnp.where(q_row == 2 * C_TERMS, -sc_ref[0], ones_q)
    for hd in range(C_HEADS):
        ck_ref[0, hd] = (ck_all[:, hd * BKV_WIDTH:(hd + 1) * BKV_WIDTH] + ones_k).astype(BF16)
        cq_ref[0, hd] = (cq_all[hd * FOX_AUG:(hd + 1) * FOX_AUG] + ones_q).astype(BF16)


def _foxc(bound, gates3, fb_row):
    bsz, seq, _ = gates3.shape
    return pl.pallas_call(
        _foxc_body,
        grid=(bsz, seq // FOXC_ROWS),
        in_specs=[
            pl.BlockSpec(memory_space=pltpu.SMEM),
            pl.BlockSpec((1, FOXC_ROWS, GATE_LANES), lambda b, i: (b, i, 0)),
            pl.BlockSpec((1, GATE_LANES), lambda b, i: (0, 0)),
        ],
        out_specs=[
            pl.BlockSpec((1, C_HEADS, FOXC_ROWS, BKV_WIDTH), lambda b, i: (b, 0, i, 0)),
            pl.BlockSpec((1, C_HEADS, FOX_AUG, FOXC_ROWS), lambda b, i: (b, 0, 0, i)),
        ],
        out_shape=[
            jax.ShapeDtypeStruct((bsz, C_HEADS, seq, BKV_WIDTH), BF16),
            jax.ShapeDtypeStruct((bsz, C_HEADS, FOX_AUG, seq), BF16),
        ],
        scratch_shapes=[pltpu.VMEM((1, GATE_LANES), F32)],
        compiler_params=_params(("parallel", "arbitrary")),
        name="fox_cumgate",
    )(bound, gates3, fb_row)


FOX_BLK = 512


def _fox_body(qi_ref, kj_ref, sc_ref, qt_ref, cq_ref, k_ref, ck_ref, vt_ref, o_ref, m_ref, acc_ref):
    qi, kj = qi_ref[pl.program_id(1)], kj_ref[pl.program_id(1)]
    blk = FOX_BLK
    use_fixed = sc_ref[1] > 0.5

    @pl.when(kj == 0)
    def _():
        m_ref[...] = jnp.full_like(m_ref, NEG_BIG)
        acc_ref[...] = jnp.zeros_like(acc_ref)

    def update(diagonal, fixed):
        pad = jnp.zeros((BKV_WIDTH - C_DH - FOX_AUG, blk), BF16)
        for hd in range(C_HEADS):
            rows = slice(hd * C_DH, (hd + 1) * C_DH)
            q_aug = jnp.concatenate([qt_ref[0, rows, :], cq_ref[0, hd], pad], axis=0)
            s = _dot(k_ref[0, hd] + ck_ref[0, hd], q_aug)
            if diagonal:
                s = jnp.where(_iota((blk, blk), 0) <= _iota((blk, blk), 1), s, NEG_BIG)
            carry = (acc_ref[hd],) if fixed else (m_ref[hd], acc_ref[hd])
            carry = _attn_step(s, _with_sum_rows(vt_ref[0, rows, :]), carry, fixed)
            acc_ref[hd] = carry[-1]
            if not fixed:
                m_ref[hd] = carry[0]
            if diagonal:
                o_ref[0, rows, :] = _attn_finish(carry).astype(o_ref.dtype)

    for fixed in (True, False):
        pick = use_fixed if fixed else jnp.logical_not(use_fixed)
        pl.when(pick & (kj < qi))(functools.partial(update, False, fixed))
        pl.when(pick & (kj == qi))(functools.partial(update, True, fixed))


def _fox(bound, fqt, cq, fk, ck, fvt):
    bsz, _, seq = fqt.shape
    nb = seq // FOX_BLK
    pairs = [(i, j) for i in range(nb) for j in range(i + 1)]
    qi = jnp.asarray([p[0] for p in pairs], jnp.int32)
    kj = jnp.asarray([p[1] for p in pairs], jnp.int32)
    q_lanes = lambda b, p, qi, kj: (b, 0, qi[p])
    k_rows = lambda b, p, qi, kj: (b, 0, kj[p], 0)
    grid_spec = pltpu.PrefetchScalarGridSpec(
        num_scalar_prefetch=2,
        grid=(bsz, len(pairs)),
        in_specs=[
            pl.BlockSpec(memory_space=pltpu.SMEM),
            pl.BlockSpec((1, C_WIDTH, FOX_BLK), q_lanes),
            pl.BlockSpec((1, C_HEADS, FOX_AUG, FOX_BLK), lambda b, p, qi, kj: (b, 0, 0, qi[p])),
            pl.BlockSpec((1, C_HEADS, FOX_BLK, BKV_WIDTH), k_rows),
            pl.BlockSpec((1, C_HEADS, FOX_BLK, BKV_WIDTH), k_rows),
            pl.BlockSpec((1, C_WIDTH, FOX_BLK), lambda b, p, qi, kj: (b, 0, kj[p])),
        ],
        out_specs=pl.BlockSpec((1, C_WIDTH, FOX_BLK), q_lanes),
        scratch_shapes=[
            pltpu.VMEM((C_HEADS, 1, FOX_BLK), F32),
            pltpu.VMEM((C_HEADS, C_DH + SUM_ROWS, FOX_BLK), F32),
        ],
    )
    return pl.pallas_call(
        _fox_body,
        grid_spec=grid_spec,
        out_shape=jax.ShapeDtypeStruct((bsz, C_WIDTH, seq), BF16),
        compiler_params=_params(("parallel", "arbitrary")),
        name="fox_attn",
    )(qi, kj, bound, fqt, cq, fk, ck, fvt)


def _cmp_body(x_ref, pos_ref, w1_ref, w2k_ref, w2vt_ref, kg_ref, cos_ref, sl_ref, sh_ref, kc_ref, vct_ref):
    n_blk = x_ref.shape[2] // CMP_STRIDE
    hw = B_GROUPS * CMP_HIDDEN
    first = [jnp.zeros((n_blk, hw), F32) for _ in range(2)]
    second = [jnp.zeros((n_blk, hw), F32) for _ in range(2)]
    for l in range(CMP_STRIDE):
        for which in range(2):
            xw = x_ref[which, 0, pl.ds(l, n_blk, stride=CMP_STRIDE), :]
            a = (xw + pos_ref[which, l:l + 1, :]).astype(BF16)
            b = (xw + pos_ref[which, CMP_STRIDE + l:CMP_STRIDE + l + 1, :]).astype(BF16)
            first[which] = first[which] + _dot(a, w1_ref[which, l])
            second[which] = second[which] + _dot(b, w1_ref[which, CMP_STRIDE + l])
    hids = []
    for which in range(2):
        hid = first[which] + pltpu.roll(second[which], n_blk - 1, axis=0)
        hid = 0.5 * hid * (1.0 + jnp.tanh(math.sqrt(2.0 / math.pi) * (hid + 0.044715 * hid * hid * hid)))
        hids.append(hid.astype(BF16))
    kc = _dot(hids[0], w2k_ref[...])
    kc = _head_rmsnorm(kc, _head_ones(BKV_WIDTH)) * kg_ref[...]
    kc = _rope(kc, cos_ref[...], sl_ref[...], sh_ref[...])
    vct = _dot_nt(w2vt_ref[...], hids[1])
    lane = _iota((n_blk, BKV_WIDTH), 1)
    for g in range(B_GROUPS):
        low = kc if g == 0 else pltpu.roll(kc, B_DH, axis=1)
        kc_ref[0, g] = jnp.where(lane < B_DH, low, jnp.where(lane == B_DH, 1.0, 0.0)).astype(BF16)
        vct_ref[0, g] = vct[g * B_DH:(g + 1) * B_DH].astype(BF16)


def _compress(cmp4, pos_e, w1_e, w2k_e, w2vt_e, kg, cos, sl, sh):
    _, bsz, seq, _ = cmp4.shape
    n_blk = seq // CMP_STRIDE
    c2 = lambda b: (0, 0)
    c3 = lambda b: (0, 0, 0)
    c4 = lambda b: (0, 0, 0, 0)
    return pl.pallas_call(
        _cmp_body,
        grid=(bsz,),
        in_specs=[
            pl.BlockSpec((2, 1, seq, BKV_WIDTH), lambda b: (0, b, 0, 0)),
            pl.BlockSpec((2, CMP_BLOCK, BKV_WIDTH), c3),
            pl.BlockSpec((2, CMP_BLOCK, BKV_WIDTH, B_GROUPS * CMP_HIDDEN), c4),
            pl.BlockSpec((B_GROUPS * CMP_HIDDEN, BKV_WIDTH), c2),
            pl.BlockSpec((BKV_WIDTH, B_GROUPS * CMP_HIDDEN), c2),
            pl.BlockSpec((1, BKV_WIDTH), c2),
            pl.BlockSpec((n_blk, BKV_WIDTH), c2),
            pl.BlockSpec((n_blk, BKV_WIDTH), c2),
            pl.BlockSpec((n_blk, BKV_WIDTH), c2),
        ],
        out_specs=[
            pl.BlockSpec((1, B_GROUPS, n_blk, BKV_WIDTH), lambda b: (b, 0, 0, 0)),
            pl.BlockSpec((1, B_GROUPS, B_DH, n_blk), lambda b: (b, 0, 0, 0)),
        ],
        out_shape=[
            jax.ShapeDtypeStruct((bsz, B_GROUPS, n_blk, BKV_WIDTH), BF16),
            jax.ShapeDtypeStruct((bsz, B_GROUPS, B_DH, n_blk), BF16),
        ],
        compiler_params=_params(("parallel",)),
        name="nsa_compress",
    )(cmp4, pos_e, w1_e, w2k_e, w2vt_e, kg, cos, sl, sh)


NSA_Q = 256
NSA_KB = 256
assert NSA_KB % NSA_Q == 0, "a query block must sit inside one key block (single diagonal step)"
SEL_UNROLL = 4


def _nsa_body(sc_ref, qt_ref, kc_ref, vct_ref, ka_ref, vt_ref, gt_ref, o_ref):
    tq = NSA_Q
    lanes = B_HPG * tq
    n_cmp = kc_ref.shape[2]
    n_sel = ka_ref.shape[2] // SEL_BLOCK
    q0 = pl.program_id(1) * tq
    t_lane = q0 + _iota((1, lanes), 1) % tq
    gates = _sigmoid(gt_ref[0])

    cn = _iota((n_sel, n_cmp), 1)
    cj = _iota((n_sel, n_cmp), 0)
    overlap_t = jnp.where((cn * CMP_STRIDE <= cj * SEL_BLOCK + SEL_BLOCK - 1)
                          & (cn * CMP_STRIDE + CMP_BLOCK - 1 >= cj * SEL_BLOCK), 1.0, 0.0).astype(BF16)
    jt = _iota((n_sel, tq), 0)
    row8 = _iota((8, tq), 0)
    tt = q0 + _iota((n_sel, tq), 1)
    cur = tt // SEL_BLOCK
    cmp_end = _iota((n_cmp, lanes), 0) * CMP_STRIDE + CMP_BLOCK - 1
    key_kb = _iota((NSA_KB, lanes), 0)
    kb_diag = q0 // NSA_KB
    offset = sc_ref[0]
    neg_offset = jnp.full((B_DH, lanes), -offset, F32).astype(BF16)

    head_rows = [[slice((g * B_HPG + hd) * B_DH, (g * B_HPG + hd + 1) * B_DH) for hd in range(B_HPG)]
                 for g in range(B_GROUPS)]
    def prepare(g, fixed):
        qt = jnp.concatenate([qt_ref[0, r, :] for r in head_rows[g]], axis=1)
        q_win = jnp.concatenate([qt, neg_offset], axis=0)

        mask_c = cmp_end <= t_lane
        s_c = jnp.where(mask_c, _dot(kc_ref[0, g], q_win), NEG_BIG)
        if fixed:
            p_c = jnp.exp2(s_c)
        else:
            p_c = jnp.where(mask_c, jnp.exp2(s_c - jnp.max(s_c, axis=0, keepdims=True)), 0.0)
        p_c = p_c * (1.0 / jnp.maximum(jnp.sum(p_c, axis=0, keepdims=True), TINY))
        o_cmp = _dot(vct_ref[0, g], p_c.astype(BF16))

        p_sum = p_c[:, 0:tq]
        for hd in range(1, B_HPG):
            p_sum = p_sum + p_c[:, hd * tq:(hd + 1) * tq]
        p_hi = p_sum.astype(BF16)
        p_lo = (p_sum - p_hi.astype(F32)).astype(BF16)
        imp = _dot(overlap_t, p_hi) + _dot(overlap_t, p_lo)
        imp = jnp.where((jt == 0) | (jt == cur) | (jt == cur - 1), POS_BIG, imp)
        imp = jnp.where(jt * SEL_BLOCK > tt, NEG_BIG, imp)
        tiles = [imp[r:r + 8] for r in range(0, n_sel, 8)]

        def count_group(t2, ranks):
            ranks = list(ranks)
            for j2 in range(8 * t2, 8 * t2 + 8):
                other = imp[j2:j2 + 1, :]
                for ti, tile in enumerate(tiles):
                    if ti > t2:
                        beats = other >= tile
                    elif ti < t2:
                        beats = other > tile
                    else:
                        beats = (other > tile) | ((other == tile) & (j2 % 8 < row8))
                    ranks[ti] = ranks[ti] + jnp.where(beats, 1.0, 0.0)
            return tuple(ranks)

        ranks = tuple(jnp.zeros((8, tq), F32) for _ in tiles)
        for t2 in range(len(tiles)):
            ranks = lax.cond(t2 * 8 * SEL_BLOCK <= q0 + tq - 1, functools.partial(count_group, t2), lambda r: r, ranks)
        rank = jnp.concatenate(ranks, axis=0)
        bias = jnp.where((rank < SEL_TOPK) & (imp > NEG_BIG * 0.5), -offset, NEG_BIG).astype(BF16)
        if n_sel < BKV_WIDTH - B_DH:
            bias = jnp.concatenate([bias, jnp.zeros((BKV_WIDTH - B_DH - n_sel, tq), BF16)], axis=0)
        q_sel = jnp.concatenate([qt, jnp.concatenate([bias] * B_HPG, axis=1)], axis=0)
        return o_cmp, q_sel, q_win

    def scores(slot, g, start, n_keys, q_aug):
        st = pl.multiple_of(start, NSA_KB)
        s = _dot(ka_ref[0, slot * B_GROUPS + g, pl.ds(st, n_keys), :], q_aug)
        vt = vt_ref[0, (slot * B_GROUPS + g) * B_DH:(slot * B_GROUPS + g + 1) * B_DH, pl.ds(st, n_keys)]
        return s, _with_sum_rows(vt)

    def gate_row(branch, g):
        base = branch * B_HEADS + g * B_HPG
        return jnp.concatenate([gates[base + hd:base + hd + 1, :] for hd in range(B_HPG)], axis=1)

    def attend(fixed):
        o_cmp, q_sel, q_win = zip(*(prepare(g, fixed) for g in range(B_GROUPS)))
        init = (_attn_init(lanes, fixed),) * B_GROUPS

        def sel_blocks(kb, carry, n_blocks):
            carry = list(carry)
            for u in range(n_blocks):
                for g in range(B_GROUPS):
                    s, vt = scores(0, g, (kb + u) * NSA_KB, NSA_KB, q_sel[g])
                    carry[g] = _attn_step(s, vt, carry[g], fixed)
            return tuple(carry)

        carry = lax.fori_loop(0, kb_diag // SEL_UNROLL, lambda i, c: sel_blocks(SEL_UNROLL * i, c, SEL_UNROLL), init)
        carry = lax.fori_loop(kb_diag - kb_diag % SEL_UNROLL, kb_diag, lambda kb, c: sel_blocks(kb, c, 1), carry)
        o_sel = []
        for g in range(B_GROUPS):
            s, vt = scores(0, g, q0, NSA_KB, q_sel[g])
            s = jnp.where(q0 + key_kb <= t_lane, s, NEG_BIG)
            o_sel.append(_attn_finish(_attn_step(s, vt, carry[g], fixed)))

        win_start = jnp.maximum(q0 - WINDOW, 0)
        kpos = win_start + _iota((WINDOW + tq, lanes), 0)
        mask_w = (kpos <= t_lane) & (kpos > t_lane - WINDOW)
        o_win = []
        for g in range(B_GROUPS):
            s, vt = scores(1, g, win_start, WINDOW + tq, q_win[g])
            o_win.append(_attn_finish(_attn_step(jnp.where(mask_w, s, NEG_BIG), vt, init[g], fixed)))

        for g in range(B_GROUPS):
            o = gate_row(0, g) * o_cmp[g] + gate_row(1, g) * o_sel[g] + gate_row(2, g) * o_win[g]
            for hd in range(B_HPG):
                o_ref[0, head_rows[g][hd], :] = o[:, hd * tq:(hd + 1) * tq].astype(o_ref.dtype)

    use_fixed = sc_ref[1] > 0.5
    pl.when(use_fixed)(functools.partial(attend, True))
    pl.when(jnp.logical_not(use_fixed))(functools.partial(attend, False))


def _nsa(bound, qt, kc, vct, ka, vt, gates_t):
    bsz, _, seq = qt.shape
    n_cmp = kc.shape[2]
    assert seq // SEL_BLOCK <= BKV_WIDTH - B_DH, "one selection-block id per spare key lane"
    assert seq >= WINDOW + NSA_Q and WINDOW % NSA_KB == 0
    whole = lambda b, i: (b, 0, 0, 0)
    return pl.pallas_call(
        _nsa_body,
        grid=(bsz, seq // NSA_Q),
        in_specs=[
            pl.BlockSpec(memory_space=pltpu.SMEM),
            pl.BlockSpec((1, BQ_WIDTH, NSA_Q), lambda b, i: (b, 0, i)),
            pl.BlockSpec((1, B_GROUPS, n_cmp, BKV_WIDTH), whole),
            pl.BlockSpec((1, B_GROUPS, B_DH, n_cmp), whole),
            pl.BlockSpec((1, 2 * B_GROUPS, seq, BKV_WIDTH), whole),
            pl.BlockSpec((1, 2 * BKV_WIDTH, seq), lambda b, i: (b, 0, 0)),
            pl.BlockSpec((1, GATE_ROWS, NSA_Q), lambda b, i: (b, 0, i)),
        ],
        out_specs=pl.BlockSpec((1, BQ_WIDTH, NSA_Q), lambda b, i: (b, 0, i)),
        out_shape=jax.ShapeDtypeStruct((bsz, BQ_WIDTH, seq), BF16),
        compiler_params=_params(("parallel", "arbitrary")),
        name="nsa_attn",
    )(bound, qt, kc, vct, ka, vt, gates_t)


FFN_ROWS = 1024
FFN_COLS = 1024


def _mix_ffn_body(x_ref, oa_ref, ob_ref, oc_ref, wo_ref, g_ref, wu_ref, wd_ref, o_ref, h_ref, acc_ref):
    j = pl.program_id(1)

    @pl.when(j == 0)
    def _():
        x = x_ref[...] + _dot(oa_ref[...], wo_ref[0:A_WIDTH, :])
        x = x + _dot_tn(ob_ref[0], wo_ref[A_WIDTH:A_WIDTH + BQ_WIDTH, :])
        x = x + _dot_tn(oc_ref[0], wo_ref[A_WIDTH + BQ_WIDTH:, :])
        var = jnp.mean(x * x, axis=-1, keepdims=True)
        h_ref[...] = (x * lax.rsqrt(var + EPS) * g_ref[...]).astype(BF16)
        acc_ref[...] = x

    u = jnp.maximum(_dot(h_ref[...], wu_ref[...]), 0.0)
    acc_ref[...] += _dot((u * u).astype(BF16), wd_ref[...])

    @pl.when(j == pl.num_programs(1) - 1)
    def _():
        o_ref[...] = acc_ref[...]


def _mix_ffn(x2d, oa2d, ob, oc, w_o, g, w_up, w_down, bsz, seq):
    n = bsz * seq
    nsb = seq // FFN_ROWS
    row = lambda i, j: (i, 0)
    const = lambda i, j: (0, 0)
    tokens_on_lanes = lambda i, j: (i // nsb, 0, i % nsb)
    return pl.pallas_call(
        _mix_ffn_body,
        grid=(n // FFN_ROWS, D_FF // FFN_COLS),
        in_specs=[
            pl.BlockSpec((FFN_ROWS, D_MODEL), row),
            pl.BlockSpec((FFN_ROWS, A_WIDTH), row),
            pl.BlockSpec((1, BQ_WIDTH, FFN_ROWS), tokens_on_lanes),
            pl.BlockSpec((1, C_WIDTH, FFN_ROWS), tokens_on_lanes),
            pl.BlockSpec((D_MODEL, D_MODEL), const),
            pl.BlockSpec((1, D_MODEL), const),
            pl.BlockSpec((D_MODEL, FFN_COLS), lambda i, j: (0, j)),
            pl.BlockSpec((FFN_COLS, D_MODEL), lambda i, j: (j, 0)),
        ],
        out_specs=pl.BlockSpec((FFN_ROWS, D_MODEL), row),
        out_shape=jax.ShapeDtypeStruct((n, D_MODEL), F32),
        scratch_shapes=[pltpu.VMEM((FFN_ROWS, D_MODEL), BF16), pltpu.VMEM((FFN_ROWS, D_MODEL), F32)],
        compiler_params=_params(("parallel", "arbitrary")),
        name="mix_ffn",
    )(x2d, oa2d, ob, oc, w_o, g, w_up, w_down)


def _rope_tables(pos, n_heads):
    inv_freq = jnp.power(jnp.float32(ROPE_THETA), -jnp.arange(0, ROT_DIM, 2, dtype=F32) / ROT_DIM)
    ang = pos.astype(F32)[:, None] * inv_freq[None, :]
    cos, sin = jnp.cos(ang), jnp.sin(ang)
    n = pos.shape[0]
    pad = jnp.zeros((n, HEAD_DIM - ROT_DIM), F32)
    zero = jnp.zeros((n, ROT_HALF), F32)
    c = jnp.concatenate([cos, cos, pad + 1.0], axis=1)
    s_lo = jnp.concatenate([zero, sin, pad], axis=1)
    s_hi = jnp.concatenate([-sin, zero, pad], axis=1)
    return tuple(jnp.tile(t, (1, n_heads)) for t in (c, s_lo, s_hi))


def _rope_tables_t(pos):
    inv_freq = jnp.power(jnp.float32(ROPE_THETA), -jnp.arange(0, ROT_DIM, 2, dtype=F32) / ROT_DIM)
    ang = pos.astype(F32)[:, None] * inv_freq[None, :]
    return jnp.cos(ang).T, jnp.sin(ang).T


def _pack_w_in(w):
    w = w.astype(BF16)
    cols = lambda lo, width: w[:, lo:lo + width]
    gates = jnp.concatenate([cols(SRC_BG, N_GATES), cols(SRC_CF, C_HEADS)], axis=1)
    pad = lambda width: jnp.zeros((D_MODEL, width - N_GATES - C_HEADS), BF16)
    w_rows = jnp.concatenate([cols(SRC_A, 4 * A_WIDTH), cols(SRC_KC, 2 * BKV_WIDTH), cols(SRC_KS, BKV_WIDTH),
                              cols(SRC_KW, BKV_WIDTH), gates, pad(GATE_LANES), cols(SRC_CK, C_WIDTH)], axis=1)
    w_lanes = jnp.concatenate([cols(SRC_BQ, BQ_WIDTH), cols(SRC_VS, BKV_WIDTH), cols(SRC_VW, BKV_WIDTH),
                               gates, pad(GATE_ROWS), cols(SRC_CQ, C_WIDTH), cols(SRC_CV, C_WIDTH)], axis=1)
    return w_rows, w_lanes.T


def _expand_groups(w):
    z = jnp.zeros_like(w)
    top = jnp.concatenate([w, z], axis=-1)
    bot = jnp.concatenate([z, w], axis=-1)
    return jnp.concatenate([top, bot], axis=-2)


def kernel(x, norm1_g, w_in, hgrn_lb_logits, hgrn_onorm_g, nsa_qn_g, nsa_kn_g, nsa_cmp_pos, nsa_cmp_w1,
           nsa_cmp_w2, fox_qn_g, fox_kn_g, fox_fb, w_o, norm2_g, w_up, w_down):
    bsz, seq, _ = x.shape
    n = bsz * seq
    n_blk = seq // CMP_STRIDE
    cos, s_lo, s_hi = _rope_tables(jnp.arange(seq), B_GROUPS)
    cos_t, sin_t = _rope_tables_t(jnp.arange(seq))
    ccos, cs_lo, cs_hi = _rope_tables(jnp.arange(n_blk) * CMP_STRIDE + CMP_BLOCK - 1, B_GROUPS)

    x2d = x.reshape(n, D_MODEL)
    for layer in range(DEPTH):
        row = lambda v, reps: jnp.tile(v[layer].astype(F32), reps)[None, :]
        w_rows, w_lanes = _pack_w_in(w_in[layer])
        hg, qt, ka, vt, cmp2d, gates, gates_t, fqt, fk, fvt = _inproj(
            x2d, norm1_g[layer][None, :], w_rows, w_lanes, cos, s_lo, s_hi, cos_t, sin_t,
            nsa_qn_g[layer].astype(F32)[:, None], row(nsa_kn_g, B_GROUPS), fox_qn_g[layer].astype(F32)[:, None],
            row(fox_kn_g, C_HEADS), bsz, seq)

        o_a = _hgrn(hg.reshape(bsz, seq, 4 * A_WIDTH), hgrn_lb_logits.astype(F32),
                    row(hgrn_onorm_g, A_HEADS), layer)

        gates3 = gates.reshape(bsz, seq, GATE_LANES)
        fb_row = jnp.zeros((1, GATE_LANES), F32).at[0, FOX_GATE_LANE0:FOX_GATE_LANE0 + C_HEADS].set(
            fox_fb[layer].astype(F32))
        fox_bound = _score_bound(fox_qn_g[layer], fox_kn_g[layer])
        ck, cq = _foxc(fox_bound, gates3, fb_row)
        o_c = _fox(fox_bound, fqt, cq, fk, ck, fvt)

        w1 = nsa_cmp_w1[layer].reshape(2, CMP_BLOCK, B_DH, CMP_HIDDEN)
        w1_e = _expand_groups(w1).astype(BF16)
        w2_e = _expand_groups(nsa_cmp_w2[layer]).astype(BF16)
        pos_e = jnp.tile(nsa_cmp_pos[layer].astype(F32), (1, 1, B_GROUPS))
        kc, vct = _compress(cmp2d.reshape(2, bsz, seq, BKV_WIDTH), pos_e, w1_e, w2_e[0], w2_e[1].T,
                            row(nsa_kn_g, B_GROUPS), ccos, cs_lo, cs_hi)
        o_b = _nsa(_score_bound(nsa_qn_g[layer], nsa_kn_g[layer]), qt, kc, vct, ka, vt, gates_t)

        x2d = _mix_ffn(x2d, o_a.reshape(n, A_WIDTH), o_b, o_c, w_o[layer].astype(BF16), norm2_g[layer][None, :],
                       w_up[layer].astype(BF16), w_down[layer].astype(BF16), bsz, seq)
    return x2d.reshape(bsz, seq, D_MODEL)
```
